```python
import jax, jax.numpy as jnp
from jax import lax
import numpy as np

D_MODEL = 1024
BATCH = 8
SEQ = 2048
DEPTH = 4

CHUNK = 64
N_EVEN = (DEPTH + 1) // 2
N_ODD = DEPTH // 2

D_POOL = D_MODEL // 2
N_POOL_GROUPS = 4
POOL_WINDOWS = (2, 4, 8, 16)
D_POOL_GROUP = D_POOL // N_POOL_GROUPS
D_CONV = D_MODEL - D_POOL
CONV_WIDTH = 31
D_EVEN_IN = D_POOL + 2 * D_CONV

HEAD_DIM = 64
D_ATTN = D_MODEL // 2
N_ATTN_HEADS = D_ATTN // HEAD_DIM
D_SC = D_MODEL - D_ATTN
SC_WIDTH = 3
Q_BLOCK = 128
D_ODD_IN = 3 * D_ATTN + N_ATTN_HEADS + 3 * D_SC

D_FF = 2816
N_EXPERTS = 8
TOP_K = 2
D_EXPERT = 3584
NORM_EPS = 1e-6

kernel_name = "hybrid_pool_conformer_fox_shortconv_moe"


def rmsnorm(x, g):
    xf = x.astype(jnp.float32)
    y = xf * lax.rsqrt(jnp.mean(xf * xf, axis=-1, keepdims=True) + NORM_EPS)
    return (y * g.astype(jnp.float32)).astype(x.dtype)


def layernorm(x, g, b):
    xf = x.astype(jnp.float32)
    mu = jnp.mean(xf, axis=-1, keepdims=True)
    xc = xf - mu
    y = xc * lax.rsqrt(jnp.mean(xc * xc, axis=-1, keepdims=True) + NORM_EPS)
    return (y * g.astype(jnp.float32) + b.astype(jnp.float32)).astype(x.dtype)


def causal_depthwise_conv(x, w):
    K, C = w.shape
    return lax.conv_general_dilated(
        x, w[:, None, :].astype(x.dtype), window_strides=(1,),
        padding=[(K - 1, 0)], dimension_numbers=("NWC", "WIO", "NWC"),
        feature_group_count=C)


def multiscale_pool(xa, w_group, scale):
    Bn, S, C = xa.shape
    xf = xa.astype(jnp.float32)
    csum = jnp.pad(jnp.cumsum(xf, axis=1), ((0, 0), (1, 0), (0, 0)))
    win = jnp.repeat(jnp.array(POOL_WINDOWS, jnp.int32), D_POOL_GROUP)
    t = jnp.arange(S, dtype=jnp.int32)[:, None]
    start = jnp.maximum(t - win[None, :] + 1, 0)
    lagged = jnp.take_along_axis(csum, jnp.broadcast_to(start[None], (Bn, S, C)), axis=1)
    count = (t - start + 1).astype(jnp.float32)
    pooled = ((csum[:, 1:] - lagged) / count - xf).astype(xa.dtype)
    pooled = pooled.reshape(Bn, S, N_POOL_GROUPS, D_POOL_GROUP)
    y = jnp.einsum("bsgc,gcd->bsgd", pooled, w_group).reshape(Bn, S, C)
    return y * scale


def conformer_conv(val, gate, dw_w, dw_b, ln_g, ln_b):
    u = val * jax.nn.sigmoid(gate)
    u = causal_depthwise_conv(u, dw_w) + dw_b
    u = layernorm(u, ln_g, ln_b)
    return jax.nn.silu(u)


def forgetting_attention(q, k, v, log_f):
    S = q.shape[2]
    F = jnp.cumsum(log_f, axis=-1)
    scale = HEAD_DIM ** -0.5
    outs = []
    for blk in range(S // Q_BLOCK):
        q0, q1 = blk * Q_BLOCK, (blk + 1) * Q_BLOCK
        s = jnp.einsum("bhqd,bhkd->bhqk", q[:, :, q0:q1], k[:, :, :q1]).astype(jnp.float32) * scale
        s = s + F[:, :, q0:q1, None] - F[:, :, None, :q1]
        mask = jnp.arange(q0, q1)[:, None] >= jnp.arange(q1)[None, :]
        p = jax.nn.softmax(jnp.where(mask, s, -jnp.inf), axis=-1)
        outs.append(jnp.einsum("bhqk,bhkd->bhqd", p.astype(v.dtype), v[:, :, :q1]))
    return jnp.concatenate(outs, axis=2)


def swiglu(h, w_gate, w_up, w_down):
    return (jax.nn.silu(h @ w_gate) * (h @ w_up)) @ w_down


def moe_swiglu(h, router_w, router_b, w_gate, w_up, w_down):
    Bn, S, D = h.shape
    ht = h.reshape(Bn * S, D)
    logits = (ht @ router_w).astype(jnp.float32) + router_b.astype(jnp.float32)
    top_val, top_idx = lax.top_k(logits, TOP_K)
    gates = jax.nn.softmax(top_val, axis=-1)
    combine = jnp.sum(jax.nn.one_hot(top_idx, N_EXPERTS, dtype=jnp.float32) * gates[..., None], axis=1)
    combine = combine.astype(h.dtype)
    out = jnp.zeros_like(ht)
    for e in range(N_EXPERTS):
        out = out + combine[:, e:e + 1] * swiglu(ht, w_gate[e], w_up[e], w_down[e])
    return out.reshape(Bn, S, D)


def setup_inputs(seed: int = 0) -> dict:
    key = jax.random.key(seed)
    ks = iter(jax.random.split(key, 32))

    def nrm(shape, scale):
        return jax.random.normal(next(ks), shape, jnp.float32) * scale

    def gain(shape):
        return 1.0 + nrm(shape, 0.05)

    E, O = N_EVEN, N_ODD
    head_offsets = jnp.linspace(1.0, 4.0, N_ATTN_HEADS, dtype=jnp.float32)
    return {
        "x": nrm((BATCH, SEQ, D_MODEL), 1.0),
        "ev_norm_mix": gain((E, D_MODEL)),
        "ev_w_in": nrm((E, D_MODEL, D_EVEN_IN), D_MODEL ** -0.5),
        "pool_w": nrm((E, N_POOL_GROUPS, D_POOL_GROUP, D_POOL_GROUP), D_POOL_GROUP ** -0.5),
        "pool_scale": gain((E, D_POOL)),
        "conv_dw_w": nrm((E, CONV_WIDTH, D_CONV), CONV_WIDTH ** -0.5),
        "conv_dw_b": nrm((E, D_CONV), 0.02),
        "conv_ln_g": gain((E, D_CONV)),
        "conv_ln_b": nrm((E, D_CONV), 0.02),
        "ev_w_out": nrm((E, D_POOL + D_CONV, D_MODEL), (D_POOL + D_CONV) ** -0.5),
        "ev_norm_ffn": gain((E, D_MODEL)),
        "ffn_w_gate": nrm((E, D_MODEL, D_FF), D_MODEL ** -0.5),
        "ffn_w_up": nrm((E, D_MODEL, D_FF), D_MODEL ** -0.5),
        "ffn_w_down": nrm((E, D_FF, D_MODEL), D_FF ** -0.5),
        "od_norm_mix": gain((O, D_MODEL)),
        "od_w_in": nrm((O, D_MODEL, D_ODD_IN), D_MODEL ** -0.5),
        "forget_bias": head_offsets[None, :] + nrm((O, N_ATTN_HEADS), 0.1),
        "q_norm_g": gain((O, HEAD_DIM)),
        "k_norm_g": gain((O, HEAD_DIM)),
        "sc_conv_w": nrm((O, SC_WIDTH, D_SC), SC_WIDTH ** -0.5),
        "od_w_out": nrm((O, D_ATTN + D_SC, D_MODEL), (D_ATTN + D_SC) ** -0.5),
        "od_norm_ffn": gain((O, D_MODEL)),
        "router_w": nrm((O, D_MODEL, N_EXPERTS), D_MODEL ** -0.5),
        "router_b": nrm((O, N_EXPERTS), 0.01),
        "moe_w_gate": nrm((O, N_EXPERTS, D_MODEL, D_EXPERT), D_MODEL ** -0.5),
        "moe_w_up": nrm((O, N_EXPERTS, D_MODEL, D_EXPERT), D_MODEL ** -0.5),
        "moe_w_down": nrm((O, N_EXPERTS, D_EXPERT, D_MODEL), D_EXPERT ** -0.5),
    }


def reference(x, ev_norm_mix, ev_w_in, pool_w, pool_scale, conv_dw_w, conv_dw_b,
              conv_ln_g, conv_ln_b, ev_w_out, ev_norm_ffn, ffn_w_gate, ffn_w_up,
              ffn_w_down, od_norm_mix, od_w_in, forget_bias, q_norm_g, k_norm_g,
              sc_conv_w, od_w_out, od_norm_ffn, router_w, router_b, moe_w_gate,
              moe_w_up, moe_w_down):
    Bn, S, D = x.shape
    for layer in range(DEPTH):
        i = layer // 2
        if layer % 2 == 0:
            h = rmsnorm(x, ev_norm_mix[i])
            proj = h @ ev_w_in[i]
            xa = proj[..., :D_POOL]
            val = proj[..., D_POOL:D_POOL + D_CONV]
            gate = proj[..., D_POOL + D_CONV:]
            ya = multiscale_pool(xa, pool_w[i], pool_scale[i])
            yb = conformer_conv(val, gate, conv_dw_w[i], conv_dw_b[i], conv_ln_g[i], conv_ln_b[i])
            x = x + jnp.concatenate([ya, yb], axis=-1) @ ev_w_out[i]
            x = x + swiglu(rmsnorm(x, ev_norm_ffn[i]), ffn_w_gate[i], ffn_w_up[i], ffn_w_down[i])
        else:
            h = rmsnorm(x, od_norm_mix[i])
            proj = h @ od_w_in[i]
            o = 0
            q = proj[..., o:o + D_ATTN]; o += D_ATTN
            k = proj[..., o:o + D_ATTN]; o += D_ATTN
            v = proj[..., o:o + D_ATTN]; o += D_ATTN
            f_logit = proj[..., o:o + N_ATTN_HEADS]; o += N_ATTN_HEADS
            sx = proj[..., o:o + D_SC]; o += D_SC
            sb = proj[..., o:o + D_SC]; o += D_SC
            sc = proj[..., o:o + D_SC]

            def heads(t):
                return t.reshape(Bn, S, N_ATTN_HEADS, HEAD_DIM).transpose(0, 2, 1, 3)

            qh = rmsnorm(heads(q), q_norm_g[i])
            kh = rmsnorm(heads(k), k_norm_g[i])
            vh = heads(v)
            log_f = jax.nn.log_sigmoid(f_logit.astype(jnp.float32)
                                       + forget_bias[i].astype(jnp.float32)).transpose(0, 2, 1)
            yc = forgetting_attention(qh, kh, vh, log_f).transpose(0, 2, 1, 3).reshape(Bn, S, D_ATTN)
            yd = sb * causal_depthwise_conv(sc * sx, sc_conv_w[i])
            x = x + jnp.concatenate([yc, yd], axis=-1) @ od_w_out[i]
            x = x + moe_swiglu(rmsnorm(x, od_norm_ffn[i]), router_w[i], router_b[i],
                               moe_w_gate[i], moe_w_up[i], moe_w_down[i])
    return x
```

```python
import functools

import jax
import jax.numpy as jnp
from jax import lax
from jax.experimental import pallas as pl
from jax.experimental.pallas import tpu as pltpu

F32 = jnp.float32
BF16 = jnp.bfloat16
I32 = jnp.int32

NORM_EPS = 1e-6
LANES = 128
POOL_WINDOWS = (2, 4, 8, 16)
CONV_WIDTH = 31
SC_WIDTH = 3
HEAD_DIM = 64
TOP_K = 2
MIX_HALO = 32
SC_HALO = 16
MOE_TILE = 1024
MOE_FF_TILE = 512
VMEM_LIMIT = 56 * 1024 * 1024


def _cparams(*sem):
    return pltpu.CompilerParams(dimension_semantics=sem, vmem_limit_bytes=VMEM_LIMIT)


def _rmsnorm(x, g):
    return x * lax.rsqrt(jnp.mean(x * x, axis=-1, keepdims=True) + NORM_EPS) * g


def _sigmoid(x):
    return 1.0 / (1.0 + jnp.exp(-x))


def _silu(x):
    return x * _sigmoid(x)


def _dot(a, b):
    return jnp.dot(a, b, preferred_element_type=F32)


def _in_even_kernel(x_ref, g_ref, w_ref, xa_ref, u_ref):
    c = xa_ref.shape[-1]
    hb = _rmsnorm(x_ref[...], g_ref[...]).astype(BF16)
    xa_ref[...] = _dot(hb, w_ref[:, 0:c]).astype(BF16)
    val = _dot(hb, w_ref[:, c:2 * c])
    gate = _dot(hb, w_ref[:, 2 * c:3 * c])
    u_ref[...] = (val * _sigmoid(gate)).astype(BF16)


def _in_even(x2, g, w, tm=512):
    T, D = x2.shape
    C = w.shape[1] // 3
    tm = min(tm, T)
    return pl.pallas_call(
        _in_even_kernel,
        grid=(T // tm,),
        in_specs=[
            pl.BlockSpec((tm, D), lambda i: (i, 0)),
            pl.BlockSpec((1, D), lambda i: (0, 0)),
            pl.BlockSpec((D, 3 * C), lambda i: (0, 0)),
        ],
        out_specs=[
            pl.BlockSpec((tm, C), lambda i: (i, 0)),
            pl.BlockSpec((tm, C), lambda i: (i, 0)),
        ],
        out_shape=[jax.ShapeDtypeStruct((T, C), BF16)] * 2,
        compiler_params=_cparams("parallel"),
        name="in_even",
    )(x2, g, w)


def _mix_even_kernel(xa_ref, xah_ref, u_ref, uh_ref, x_ref, pw_ref, ps_ref, dww_ref,
                     dwb_ref, lng_ref, lnb_ref, wo_ref, o_ref, xa_ext, u_ext, mix):
    i = pl.program_id(1)
    ts = xa_ref.shape[1]
    C = xa_ref.shape[2]
    H = MIX_HALO

    @pl.when(i == 0)
    def _():
        xa_ext[0:H, :] = jnp.zeros((H, C), F32)
        u_ext[0:H, :] = jnp.zeros((H, C), F32)

    @pl.when(i > 0)
    def _():
        xa_ext[0:H, :] = xah_ref[0].astype(F32)
        u_ext[0:H, :] = uh_ref[0].astype(F32)

    xa_ext[H:H + ts, :] = xa_ref[0].astype(F32)
    u_ext[H:H + ts, :] = u_ref[0].astype(F32)

    t_glob = i * ts + lax.broadcasted_iota(I32, (ts, 1), 0)
    for g, win in enumerate(POOL_WINDOWS):
        lo = g * LANES
        acc = xa_ext[H:H + ts, lo:lo + LANES]
        cur = acc
        for j in range(1, win):
            acc = acc + xa_ext[H - j:H - j + ts, lo:lo + LANES]
        cnt = jnp.minimum(t_glob + 1, win).astype(F32)
        pooled = acc / cnt - cur
        y = _dot(pooled.astype(BF16), pw_ref[g]) * ps_ref[:, lo:lo + LANES]
        mix[:, lo:lo + LANES] = y.astype(BF16)

    CH = 64
    for r0 in range(0, ts, CH):
        acc = jnp.zeros((CH, C), F32) + dwb_ref[...]
        for k in range(CONV_WIDTH):
            off = H - (CONV_WIDTH - 1) + k + r0
            acc = acc + dww_ref[k:k + 1, :] * u_ext[off:off + CH, :]
        mu = jnp.mean(acc, axis=-1, keepdims=True)
        xc = acc - mu
        var = jnp.mean(xc * xc, axis=-1, keepdims=True)
        yn = xc * lax.rsqrt(var + NORM_EPS) * lng_ref[...] + lnb_ref[...]
        mix[r0:r0 + CH, C:2 * C] = _silu(yn).astype(BF16)

    o_ref[0] = x_ref[0] + _dot(mix[...], wo_ref[...])


def _mix_even(xa, u, x, pw, ps, dww, dwb, lng, lnb, wo, ts=512):
    B, S, C = xa.shape
    D = x.shape[-1]
    ts = min(ts, S)
    hb = ts // MIX_HALO
    main = lambda b, i: (b, i, 0)
    halo = lambda b, i: (b, jnp.maximum(i * hb - 1, 0), 0)
    const2 = lambda b, i: (0, 0)
    return pl.pallas_call(
        _mix_even_kernel,
        grid=(B, S // ts),
        in_specs=[
            pl.BlockSpec((1, ts, C), main),
            pl.BlockSpec((1, MIX_HALO, C), halo),
            pl.BlockSpec((1, ts, C), main),
            pl.BlockSpec((1, MIX_HALO, C), halo),
            pl.BlockSpec((1, ts, D), main),
            pl.BlockSpec(pw.shape, lambda b, i: (0, 0, 0)),
            pl.BlockSpec((1, C), const2),
            pl.BlockSpec(dww.shape, const2),
            pl.BlockSpec((1, C), const2),
            pl.BlockSpec((1, C), const2),
            pl.BlockSpec((1, C), const2),
            pl.BlockSpec(wo.shape, const2),
        ],
        out_specs=pl.BlockSpec((1, ts, D), main),
        out_shape=jax.ShapeDtypeStruct((B, S, D), F32),
        scratch_shapes=[
            pltpu.VMEM((MIX_HALO + ts, C), F32),
            pltpu.VMEM((MIX_HALO + ts, C), F32),
            pltpu.VMEM((ts, 2 * C), BF16),
        ],
        compiler_params=_cparams("parallel", "arbitrary"),
        name="mix_even",
    )(xa, xa, u, u, x, pw, ps, dww, dwb, lng, lnb, wo)


def _ffn_kernel(x_ref, g_ref, wg_ref, wu_ref, wd_ref, o_ref, h_scr, act_scr, *, fc):
    x = x_ref[...]
    h_scr[...] = _rmsnorm(x, g_ref[...]).astype(BF16)
    F = wg_ref.shape[1]
    for c0 in range(0, F, fc):
        hb = h_scr[...]
        a = _dot(hb, wg_ref[:, c0:c0 + fc])
        b = _dot(hb, wu_ref[:, c0:c0 + fc])
        act_scr[:, c0:c0 + fc] = (_silu(a) * b).astype(BF16)
    o_ref[...] = x + _dot(act_scr[...], wd_ref[...])


def _ffn(x2, g, wg, wu, wd, tm=512, fc=256):
    T, D = x2.shape
    F = wg.shape[1]
    tm = min(tm, T)
    fc = min(fc, F)
    const = lambda i: (0, 0)
    return pl.pallas_call(
        functools.partial(_ffn_kernel, fc=fc),
        grid=(T // tm,),
        in_specs=[
            pl.BlockSpec((tm, D), lambda i: (i, 0)),
            pl.BlockSpec((1, D), const),
            pl.BlockSpec((D, F), const),
            pl.BlockSpec((D, F), const),
            pl.BlockSpec((F, D), const),
        ],
        out_specs=pl.BlockSpec((tm, D), lambda i: (i, 0)),
        out_shape=jax.ShapeDtypeStruct((T, D), F32),
        scratch_shapes=[pltpu.VMEM((tm, D), BF16), pltpu.VMEM((tm, F), BF16)],
        compiler_params=_cparams("parallel"),
        name="ffn",
    )(x2, g, wg, wu, wd)


def _in_odd_kernel(x_ref, g_ref, w_ref, wf_ref, fb_ref, bd_ref, gq_ref, gk_ref,
                   q_ref, k_ref, v_ref, lf_ref, u_ref, sb_ref):
    A = q_ref.shape[-1]
    hb = _rmsnorm(x_ref[...], g_ref[...]).astype(BF16)

    def headnorm(t, gain):
        ss = _dot((t * t).astype(BF16), bd_ref[...])
        return t * lax.rsqrt(ss * (1.0 / HEAD_DIM) + NORM_EPS) * gain

    q = _dot(hb, w_ref[:, 0:A])
    q_ref[...] = (headnorm(q, gq_ref[...]) * (HEAD_DIM ** -0.5)).astype(BF16)
    k = _dot(hb, w_ref[:, A:2 * A])
    k_ref[...] = headnorm(k, gk_ref[...]).astype(BF16)
    v_ref[...] = _dot(hb, w_ref[:, 2 * A:3 * A]).astype(BF16)
    z = _dot(hb, wf_ref[...]) + fb_ref[...]
    lf_ref[...] = jnp.minimum(z, 0.0) - jnp.log1p(jnp.exp(-jnp.abs(z)))
    sx = _dot(hb, w_ref[:, 3 * A:4 * A])
    sb_ref[...] = _dot(hb, w_ref[:, 4 * A:5 * A]).astype(BF16)
    sc = _dot(hb, w_ref[:, 5 * A:6 * A])
    u_ref[...] = (sc * sx).astype(BF16)


def _in_odd(x2, g, w, wf, fb, bd, gq, gk, tm=512):
    T, D = x2.shape
    A = w.shape[1] // 6
    tm = min(tm, T)
    const = lambda i: (0, 0)
    row = lambda i: (i, 0)
    return pl.pallas_call(
        _in_odd_kernel,
        grid=(T // tm,),
        in_specs=[
            pl.BlockSpec((tm, D), row),
            pl.BlockSpec((1, D), const),
            pl.BlockSpec(w.shape, const),
            pl.BlockSpec(wf.shape, const),
            pl.BlockSpec((1, LANES), const),
            pl.BlockSpec(bd.shape, const),
            pl.BlockSpec((1, A), const),
            pl.BlockSpec((1, A), const),
        ],
        out_specs=[
            pl.BlockSpec((tm, A), row), pl.BlockSpec((tm, A), row), pl.BlockSpec((tm, A), row),
            pl.BlockSpec((tm, LANES), row),
            pl.BlockSpec((tm, A), row), pl.BlockSpec((tm, A), row),
        ],
        out_shape=[
            jax.ShapeDtypeStruct((T, A), BF16), jax.ShapeDtypeStruct((T, A), BF16),
            jax.ShapeDtypeStruct((T, A), BF16), jax.ShapeDtypeStruct((T, LANES), F32),
            jax.ShapeDtypeStruct((T, A), BF16), jax.ShapeDtypeStruct((T, A), BF16),
        ],
        compiler_params=_cparams("parallel"),
        name="in_odd",
    )(x2, g, w, wf, fb, bd, gq, gk)


def _attn_kernel(q_ref, k_ref, v_ref, lf_ref, o_ref, fq_scr, fk_scr, *, tq):
    p = pl.program_id(1)
    qi = pl.program_id(2)
    S = k_ref.shape[1]
    lane = lax.broadcasted_iota(I32, (1, LANES), 1)

    @pl.when(qi == 0)
    def _():
        f = lf_ref[0]
        row = lax.broadcasted_iota(I32, (S, 1), 0)
        sh = 1
        while sh < S:
            f = f + jnp.where(row >= sh, pltpu.roll(f, sh, axis=0), 0.0)
            sh *= 2
        ft = f.T
        sub = lax.broadcasted_iota(I32, (LANES, 1), 0)
        for hl in range(2):
            hg = 2 * p + hl
            col = jnp.sum(jnp.where(lane == hg, f, 0.0), axis=-1, keepdims=True)
            fq_scr[hl] = jnp.broadcast_to(col, (S, LANES))
            fk_scr[hl:hl + 1, :] = jnp.sum(jnp.where(sub == hg, ft, 0.0), axis=0, keepdims=True)

    q0 = pl.multiple_of(qi * tq, tq)
    qb = q_ref[0]
    rowi = lax.broadcasted_iota(I32, (tq, tq), 0)
    coli = lax.broadcasted_iota(I32, (tq, tq), 1)
    causal = rowi >= coli
    nrep = tq // LANES
    outs = []
    for hl in range(2):
        hmask = (lane >= hl * HEAD_DIM) & (lane < (hl + 1) * HEAD_DIM)
        qh = jnp.where(hmask, qb, jnp.zeros_like(qb))
        fq = fq_scr[hl, pl.ds(q0, tq), :]
        fqt = jnp.concatenate([fq] * nrep, axis=1)

        def scores(k0):
            kb = k_ref[0, pl.ds(k0, tq), :]
            s = lax.dot_general(qh, kb, (((1,), (1,)), ((), ())), preferred_element_type=F32)
            return s + fqt - fk_scr[hl:hl + 1, pl.ds(k0, tq)]

        s = jnp.where(causal, scores(q0), -jnp.inf)
        m = jnp.max(s, axis=-1, keepdims=True)
        pr = jnp.exp(s - m)
        l = jnp.sum(pr, axis=-1, keepdims=True)
        acc = _dot(pr.astype(BF16), v_ref[0, pl.ds(q0, tq), :])

        def body(j, carry):
            m, l, acc = carry
            k0 = pl.multiple_of(j * tq, tq)
            s = scores(k0)
            m_new = jnp.maximum(m, jnp.max(s, axis=-1, keepdims=True))
            alpha = jnp.exp(m - m_new)
            pr = jnp.exp(s - m_new)
            l = alpha * l + jnp.sum(pr, axis=-1, keepdims=True)
            acc = alpha * acc + _dot(pr.astype(BF16), v_ref[0, pl.ds(k0, tq), :])
            return m_new, l, acc

        m, l, acc = lax.fori_loop(0, qi, body, (m, l, acc))
        outs.append(acc / l)
    o_ref[0] = jnp.where(lane < HEAD_DIM, outs[0], outs[1]).astype(BF16)


def _attn(q, k, v, lf, tq=512):
    B, S, A = q.shape
    tq = min(tq, S)
    npair = A // LANES
    return pl.pallas_call(
        functools.partial(_attn_kernel, tq=tq),
        grid=(B, npair, S // tq),
        in_specs=[
            pl.BlockSpec((1, tq, LANES), lambda b, p, i: (b, i, p)),
            pl.BlockSpec((1, S, LANES), lambda b, p, i: (b, 0, p)),
            pl.BlockSpec((1, S, LANES), lambda b, p, i: (b, 0, p)),
            pl.BlockSpec((1, S, LANES), lambda b, p, i: (b, 0, 0)),
        ],
        out_specs=pl.BlockSpec((1, tq, LANES), lambda b, p, i: (b, i, p)),
        out_shape=jax.ShapeDtypeStruct((B, S, A), BF16),
        scratch_shapes=[pltpu.VMEM((2, S, LANES), F32), pltpu.VMEM((8, S), F32)],
        compiler_params=_cparams("parallel", "parallel", "arbitrary"),
        name="attn",
    )(q, k, v, lf)


def _out_odd_kernel(yc_ref, u_ref, uh_ref, sb_ref, x_ref, cw_ref, wo_ref, o_ref, u_ext, mix):
    i = pl.program_id(1)
    ts = u_ref.shape[1]
    C = u_ref.shape[2]
    H = SC_HALO

    @pl.when(i == 0)
    def _():
        u_ext[0:H, :] = jnp.zeros((H, C), F32)

    @pl.when(i > 0)
    def _():
        u_ext[0:H, :] = uh_ref[0].astype(F32)

    u_ext[H:H + ts, :] = u_ref[0].astype(F32)
    conv = jnp.zeros((ts, C), F32)
    for k in range(SC_WIDTH):
        off = H - (SC_WIDTH - 1) + k
        conv = conv + cw_ref[k:k + 1, :] * u_ext[off:off + ts, :]
    mix[:, 0:C] = yc_ref[0]
    mix[:, C:2 * C] = (sb_ref[0].astype(F32) * conv).astype(BF16)
    o_ref[0] = x_ref[0] + _dot(mix[...], wo_ref[...])


def _out_odd(yc, u, sb, x, cw, wo, ts=512):
    B, S, C = u.shape
    D = x.shape[-1]
    ts = min(ts, S)
    hb = ts // SC_HALO
    main = lambda b, i: (b, i, 0)
    halo = lambda b, i: (b, jnp.maximum(i * hb - 1, 0), 0)
    const2 = lambda b, i: (0, 0)
    return pl.pallas_call(
        _out_odd_kernel,
        grid=(B, S // ts),
        in_specs=[
            pl.BlockSpec((1, ts, C), main),
            pl.BlockSpec((1, ts, C), main),
            pl.BlockSpec((1, SC_HALO, C), halo),
            pl.BlockSpec((1, ts, C), main),
            pl.BlockSpec((1, ts, D), main),
            pl.BlockSpec(cw.shape, const2),
            pl.BlockSpec(wo.shape, const2),
        ],
        out_specs=pl.BlockSpec((1, ts, D), main),
        out_shape=jax.ShapeDtypeStruct((B, S, D), F32),
        scratch_shapes=[pltpu.VMEM((SC_HALO + ts, C), F32), pltpu.VMEM((ts, 2 * C), BF16)],
        compiler_params=_cparams("parallel", "arbitrary"),
        name="out_odd",
    )(yc, u, u, sb, x, cw, wo)


def _router_kernel(x_ref, g_ref, rw_ref, rb_ref, meta_ref, cnt_ref, base_scr, *, n_exp):
    i = pl.program_id(0)
    tm = x_ref.shape[0]

    @pl.when(i == 0)
    def _():
        base_scr[...] = jnp.zeros_like(base_scr)

    h = _rmsnorm(x_ref[...], g_ref[...])
    lane = lax.broadcasted_iota(I32, (1, LANES), 1)
    logits = jnp.full((tm, LANES), -jnp.inf, F32)
    for e in range(n_exp):
        s = jnp.sum(h * rw_ref[e:e + 1, :], axis=-1, keepdims=True)
        logits = jnp.where(lane == e, s, logits)
    logits = logits + rb_ref[...]

    m1 = jnp.max(logits, axis=-1, keepdims=True)
    i1 = jnp.min(jnp.where(logits == m1, lane, LANES), axis=-1, keepdims=True)
    rest = jnp.where(lane == i1, -jnp.inf, logits)
    m2 = jnp.max(rest, axis=-1, keepdims=True)
    i2 = jnp.min(jnp.where(rest == m2, lane, LANES), axis=-1, keepdims=True)
    e2 = jnp.exp(m2 - m1)
    g1 = 1.0 / (1.0 + e2)
    g2 = e2 / (1.0 + e2)

    oh1 = (lane == i1)
    oh2 = (lane == i2)
    sel = jnp.where(oh1 | oh2, 1.0, 0.0).astype(BF16)
    r = lax.broadcasted_iota(I32, (tm, tm), 0)
    c = lax.broadcasted_iota(I32, (tm, tm), 1)
    strict = jnp.where(r > c, 1.0, 0.0).astype(BF16)
    before = _dot(strict, sel) + base_scr[...]
    r1 = jnp.sum(jnp.where(oh1, before, 0.0), axis=-1, keepdims=True)
    r2 = jnp.sum(jnp.where(oh2, before, 0.0), axis=-1, keepdims=True)
    base_scr[...] = base_scr[...] + jnp.sum(sel.astype(F32), axis=0, keepdims=True)

    meta = jnp.zeros((tm, LANES), F32)
    for idx, val in enumerate((i1.astype(F32), i2.astype(F32), g1, g2, r1, r2)):
        meta = jnp.where(lane == idx, val, meta)
    meta_ref[...] = meta
    cnt_ref[...] = base_scr[...]


def _router(x2, g, rwt, rb, n_exp, tm=512):
    T, D = x2.shape
    tm = min(tm, T)
    const = lambda i: (0, 0)
    return pl.pallas_call(
        functools.partial(_router_kernel, n_exp=n_exp),
        grid=(T // tm,),
        in_specs=[
            pl.BlockSpec((tm, D), lambda i: (i, 0)),
            pl.BlockSpec((1, D), const),
            pl.BlockSpec(rwt.shape, const),
            pl.BlockSpec((1, LANES), const),
        ],
        out_specs=[pl.BlockSpec((tm, LANES), lambda i: (i, 0)), pl.BlockSpec((1, LANES), const)],
        out_shape=[jax.ShapeDtypeStruct((T, LANES), F32), jax.ShapeDtypeStruct((1, LANES), F32)],
        scratch_shapes=[pltpu.VMEM((1, LANES), F32)],
        compiler_params=_cparams("arbitrary"),
        name="router",
    )(x2, g, rwt, rb)


def _dispatch_kernel(pos_ref, pad_ref, x_hbm, xs_hbm, zbuf, sem, zsem, *, n_exp, chunk, tile):
    T = x_hbm.shape[0]
    zbuf[...] = jnp.zeros_like(zbuf)

    def row_copy(t, slot):
        return pltpu.make_async_copy(x_hbm.at[pl.ds(t, 1)], xs_hbm.at[pl.ds(slot, 1)], sem)

    def chunk_wait():
        pltpu.make_async_copy(x_hbm.at[pl.ds(0, chunk)], xs_hbm.at[pl.ds(0, chunk)], sem).wait()

    def zero_copies(e):
        start, head, rem = pad_ref[3 * e], pad_ref[3 * e + 1], pad_ref[3 * e + 2]
        out = []
        for r in range(7):
            out.append((r < head, pltpu.make_async_copy(
                zbuf.at[pl.ds(0, 1)], xs_hbm.at[pl.ds(start + r, 1)], zsem)))
        for b in range(3, (tile - 1).bit_length()):
            n = 1 << b
            dst = pl.multiple_of(start + head + (rem & (n - 1)), 8)
            out.append((((rem >> b) & 1) == 1, pltpu.make_async_copy(
                zbuf.at[pl.ds(0, n)], xs_hbm.at[pl.ds(dst, n)], zsem)))
        return out

    def tail_copies():
        zr = zbuf.shape[0]
        n_active = pad_ref[3 * n_exp]
        out = []
        for t in range(xs_hbm.shape[0] // tile):
            for r0 in range(0, tile, zr):
                out.append((t >= n_active, pltpu.make_async_copy(
                    zbuf, xs_hbm.at[pl.ds(t * tile + r0, zr)], zsem)))
        return out

    def all_zero_copies():
        return [pc for e in range(n_exp) for pc in zero_copies(e)] + tail_copies()

    for pred, cp in all_zero_copies():
        @pl.when(pred)
        def _():
            cp.start()

    def chunk_body(c, _):
        def tok(r, _):
            t = c * chunk + r
            row_copy(t, pos_ref[t]).start()
            row_copy(t, pos_ref[T + t]).start()
            return 0
        lax.fori_loop(0, chunk, tok, 0)

        @pl.when(c > 0)
        def _():
            chunk_wait()
            chunk_wait()
        return 0

    lax.fori_loop(0, T // chunk, chunk_body, 0)
    chunk_wait()
    chunk_wait()

    for pred, cp in all_zero_copies():
        @pl.when(pred)
        def _():
            cp.wait()


def _dispatch(pos, pad, x2, n_rows, n_exp, tile):
    T, D = x2.shape
    chunk = min(1024, T)
    return pl.pallas_call(
        functools.partial(_dispatch_kernel, n_exp=n_exp, chunk=chunk, tile=tile),
        grid_spec=pltpu.PrefetchScalarGridSpec(
            num_scalar_prefetch=2,
            grid=(1,),
            in_specs=[pl.BlockSpec(memory_space=pl.ANY)],
            out_specs=pl.BlockSpec(memory_space=pl.ANY),
            scratch_shapes=[
                pltpu.VMEM((max(tile // 2, 8), D), F32),
                pltpu.SemaphoreType.DMA,
                pltpu.SemaphoreType.DMA,
            ],
        ),
        out_shape=jax.ShapeDtypeStruct((n_rows, D), F32),
        compiler_params=_cparams("arbitrary"),
        name="dispatch",
    )(pos, pad, x2)


def _experts_kernel(te_ref, na_ref, xs_ref, g_ref, wg_ref, wu_ref, wd_ref, y_ref, h_scr, acc_scr):
    t = pl.program_id(0)
    j = pl.program_id(1)
    nj = pl.num_programs(1)

    @pl.when(t < na_ref[0])
    def _():
        @pl.when(j == 0)
        def _():
            h_scr[...] = _rmsnorm(xs_ref[...], g_ref[...]).astype(BF16)
            acc_scr[...] = jnp.zeros_like(acc_scr)

        hb = h_scr[...]
        a = _dot(hb, wg_ref[0].astype(BF16))
        b = _dot(hb, wu_ref[0].astype(BF16))
        act = (_silu(a) * b).astype(BF16)
        acc_scr[...] += _dot(act, wd_ref[0].astype(BF16))

        @pl.when(j == nj - 1)
        def _():
            y_ref[...] = acc_scr[...]

    @pl.when((t >= na_ref[0]) & (j == 0))
    def _():
        y_ref[...] = jnp.zeros_like(y_ref)


def _experts(tile_exp, n_active, xs, g, wg, wu, wd, tile, tf):
    P, D = xs.shape
    E, _, F = wg.shape
    tf = min(tf, F)
    nj = F // tf
    n_tiles = P // tile

    def row_map(t, j, te, na):
        return (jnp.minimum(t, na[0] - 1), 0)

    def jeff(t, j, na):
        return jnp.where(t < na[0], j, nj - 1)

    return pl.pallas_call(
        _experts_kernel,
        grid_spec=pltpu.PrefetchScalarGridSpec(
            num_scalar_prefetch=2,
            grid=(n_tiles, nj),
            in_specs=[
                pl.BlockSpec((tile, D), row_map),
                pl.BlockSpec((1, D), lambda t, j, te, na: (0, 0)),
                pl.BlockSpec((1, D, tf), lambda t, j, te, na: (te[t], 0, jeff(t, j, na))),
                pl.BlockSpec((1, D, tf), lambda t, j, te, na: (te[t], 0, jeff(t, j, na))),
                pl.BlockSpec((1, tf, D), lambda t, j, te, na: (te[t], jeff(t, j, na), 0)),
            ],
            out_specs=pl.BlockSpec((tile, D), lambda t, j, te, na: (t, 0)),
            scratch_shapes=[pltpu.VMEM((tile, D), BF16), pltpu.VMEM((tile, D), F32)],
        ),
        out_shape=jax.ShapeDtypeStruct((P, D), F32),
        compiler_params=_cparams("arbitrary", "arbitrary"),
        name="experts",
    )(tile_exp, n_active, xs, g, wg, wu, wd)


def _combine_kernel(pos_ref, x_ref, meta_ref, y_hbm, o_ref, buf, sem):
    i = pl.program_id(0)
    tc = x_ref.shape[0]
    T = pos_ref.shape[0] // TOP_K
    base = i * tc

    def issue(r, _):
        for k in range(TOP_K):
            slot = pos_ref[k * T + base + r]
            pltpu.make_async_copy(y_hbm.at[pl.ds(slot, 1)], buf.at[k, pl.ds(r, 1)], sem).start()
        return 0

    lax.fori_loop(0, tc, issue, 0)
    for k in range(TOP_K):
        pltpu.make_async_copy(y_hbm.at[pl.ds(0, tc)], buf.at[k], sem).wait()
    meta = meta_ref[...]
    o_ref[...] = x_ref[...] + meta[:, 2:3] * buf[0] + meta[:, 3:4] * buf[1]


def _combine(pos, x2, meta, y, tc=512):
    T, D = x2.shape
    tc = min(tc, T)
    return pl.pallas_call(
        _combine_kernel,
        grid_spec=pltpu.PrefetchScalarGridSpec(
            num_scalar_prefetch=1,
            grid=(T // tc,),
            in_specs=[
                pl.BlockSpec((tc, D), lambda i, pos: (i, 0)),
                pl.BlockSpec((tc, LANES), lambda i, pos: (i, 0)),
                pl.BlockSpec(memory_space=pl.ANY),
            ],
            out_specs=pl.BlockSpec((tc, D), lambda i, pos: (i, 0)),
            scratch_shapes=[pltpu.VMEM((TOP_K, tc, D), F32), pltpu.SemaphoreType.DMA],
        ),
        out_shape=jax.ShapeDtypeStruct((T, D), F32),
        compiler_params=_cparams("arbitrary"),
        name="combine",
    )(pos, x2, meta, y)


def _moe(x2, g, rw, rb, wg, wu, wd):
    T, D = x2.shape
    E = rw.shape[1]
    tile = min(MOE_TILE, T)
    n_tiles = (T * TOP_K) // tile + E
    rwt = jnp.zeros((8 * ((E + 7) // 8), D), F32).at[:E].set(rw.T)
    rbp = jnp.zeros((1, LANES), F32).at[0, :E].set(rb)
    meta, cnt = _router(x2, g, rwt, rbp, E)

    counts = cnt[0, :E].astype(I32)
    tiles_e = (counts + tile - 1) // tile
    tile_end = jnp.cumsum(tiles_e)
    row_off = (tile_end - tiles_e) * tile
    n_active = tile_end[-1:]
    tile_exp = jnp.minimum(jnp.searchsorted(tile_end, jnp.arange(n_tiles, dtype=I32), side="right"), E - 1)
    tile_exp = jnp.where(jnp.arange(n_tiles) < n_active[0], tile_exp, tile_exp[jnp.maximum(n_active[0] - 1, 0)])
    pad_start = row_off + counts
    pad_head = (-pad_start) & 7
    pad_rest = tiles_e * tile - counts - pad_head
    pad = jnp.concatenate([jnp.stack([pad_start, pad_head, pad_rest], axis=1).reshape(-1), n_active]).astype(I32)
    eid = meta[:, 0:TOP_K].astype(I32)
    rank = meta[:, 4:4 + TOP_K].astype(I32)
    pos = (row_off[eid] + rank).T.reshape(-1)

    xs = _dispatch(pos, pad, x2, n_tiles * tile, E, tile)
    y = _experts(tile_exp.astype(I32), n_active.astype(I32), xs, g, wg, wu, wd, tile, MOE_FF_TILE)
    return _combine(pos, x2, meta, y)


def kernel(x, ev_norm_mix, ev_w_in, pool_w, pool_scale, conv_dw_w, conv_dw_b, conv_ln_g, conv_ln_b, ev_w_out, ev_norm_ffn, ffn_w_gate, ffn_w_up, ffn_w_down, od_norm_mix, od_w_in, forget_bias, q_norm_g, k_norm_g, sc_conv_w, od_w_out, od_norm_ffn, router_w, router_b, moe_w_gate, moe_w_up, moe_w_down):
    B, S, D = x.shape
    T = B * S
    depth = ev_norm_mix.shape[0] + od_norm_mix.shape[0]
    A = od_w_in.shape[2] * HEAD_DIM // (6 * HEAD_DIM + 1)
    H = A // HEAD_DIM
    row = lambda v: v.reshape(1, -1)
    head = jnp.arange(A) // HEAD_DIM
    bd = (head[:, None] == head[None, :]).astype(BF16)

    x2 = x.reshape(T, D)
    for layer in range(depth):
        i = layer // 2
        if layer % 2 == 0:
            C = pool_scale.shape[1]
            xa, u = _in_even(x2, row(ev_norm_mix[i]), ev_w_in[i].astype(BF16))
            x3 = _mix_even(
                xa.reshape(B, S, C), u.reshape(B, S, C), x2.reshape(B, S, D),
                pool_w[i].astype(BF16), row(pool_scale[i]), conv_dw_w[i], row(conv_dw_b[i]),
                row(conv_ln_g[i]), row(conv_ln_b[i]), ev_w_out[i].astype(BF16))
            x2 = _ffn(x3.reshape(T, D), row(ev_norm_ffn[i]), ffn_w_gate[i].astype(BF16),
                      ffn_w_up[i].astype(BF16), ffn_w_down[i].astype(BF16))
        else:
            w = od_w_in[i]
            w_main = jnp.concatenate([w[:, :3 * A], w[:, 3 * A + H:]], axis=1).astype(BF16)
            w_f = jnp.zeros((D, LANES), F32).at[:, :H].set(w[:, 3 * A:3 * A + H]).astype(BF16)
            fb = jnp.zeros((1, LANES), F32).at[0, :H].set(forget_bias[i])
            q, k, v, lf, u, sb = _in_odd(
                x2, row(od_norm_mix[i]), w_main, w_f, fb, bd,
                row(jnp.tile(q_norm_g[i], H)), row(jnp.tile(k_norm_g[i], H)))
            sh = lambda t: t.reshape(B, S, -1)
            yc = _attn(sh(q), sh(k), sh(v), sh(lf))
            x3 = _out_odd(yc, sh(u), sh(sb), x2.reshape(B, S, D), sc_conv_w[i], od_w_out[i].astype(BF16))
            x2 = _moe(x3.reshape(T, D), row(od_norm_ffn[i]), router_w[i], router_b[i],
                      moe_w_gate[i], moe_w_up[i], moe_w_down[i])
    return x2.reshape(B, S, D)
```

```python
import functools

import jax
import jax.numpy as jnp
from jax import lax
from jax.experimental import pallas as pl
from jax.experimental.pallas import tpu as pltpu

F32 = jnp.float32
BF16 = jnp.bfloat16
I32 = jnp.int32

NORM_EPS = 1e-6
LANES = 128
POOL_WINDOWS = (2, 4, 8, 16)
CONV_WIDTH = 31
SC_WIDTH = 3
HEAD_DIM = 64
TOP_K = 2
MIX_HALO = 32
SC_HALO = 16
MOE_TILE = 1024
MOE_FF_TILE = 512
VMEM_LIMIT = 56 * 1024 * 1024


def _cparams(*sem):
    return pltpu.CompilerParams(dimension_semantics=sem, vmem_limit_bytes=VMEM_LIMIT)


def _rmsnorm(x, g):
    return x * lax.rsqrt(jnp.mean(x * x, axis=-1, keepdims=True) + NORM_EPS) * g


def _sigmoid(x):
    return 1.0 / (1.0 + jnp.exp(-x))


def _silu(x):
    return x * _sigmoid(x)


def _dot(a, b):
    return jnp.dot(a, b, preferred_element_type=F32)


def _in_even_kernel(x_ref, g_ref, w_ref, xa_ref, u_ref):
    c = xa_ref.shape[-1]
    hb = _rmsnorm(x_ref[...], g_ref[...]).astype(BF16)
    xa_ref[...] = _dot(hb, w_ref[:, 0:c]).astype(BF16)
    val = _dot(hb, w_ref[:, c:2 * c])
    gate = _dot(hb, w_ref[:, 2 * c:3 * c])
    u_ref[...] = (val * _sigmoid(gate)).astype(BF16)


def _in_even(x2, g, w, tm=512):
    T, D = x2.shape
    C = w.shape[1] // 3
    tm = min(tm, T)
    return pl.pallas_call(
        _in_even_kernel,
        grid=(T // tm,),
        in_specs=[
            pl.BlockSpec((tm, D), lambda i: (i, 0)),
            pl.BlockSpec((1, D), lambda i: (0, 0)),
            pl.BlockSpec((D, 3 * C), lambda i: (0, 0)),
        ],
        out_specs=[
            pl.BlockSpec((tm, C), lambda i: (i, 0)),
            pl.BlockSpec((tm, C), lambda i: (i, 0)),
        ],
        out_shape=[jax.ShapeDtypeStruct((T, C), BF16)] * 2,
        compiler_params=_cparams("parallel"),
        name="in_even",
    )(x2, g, w)


def _mix_even_kernel(xa_ref, xah_ref, u_ref, uh_ref, x_ref, pw_ref, ps_ref, dww_ref,
                     dwb_ref, lng_ref, lnb_ref, wo_ref, o_ref, xa_ext, u_ext, mix):
    i = pl.program_id(1)
    ts = xa_ref.shape[1]
    C = xa_ref.shape[2]
    H = MIX_HALO

    @pl.when(i == 0)
    def _():
        xa_ext[0:H, :] = jnp.zeros((H, C), F32)
        u_ext[0:H, :] = jnp.zeros((H, C), F32)

    @pl.when(i > 0)
    def _():
        xa_ext[0:H, :] = xah_ref[0].astype(F32)
        u_ext[0:H, :] = uh_ref[0].astype(F32)

    xa_ext[H:H + ts, :] = xa_ref[0].astype(F32)
    u_ext[H:H + ts, :] = u_ref[0].astype(F32)

    t_glob = i * ts + lax.broadcasted_iota(I32, (ts, 1), 0)
    for g, win in enumerate(POOL_WINDOWS):
        lo = g * LANES
        acc = xa_ext[H:H + ts, lo:lo + LANES]
        cur = acc
        for j in range(1, win):
            acc = acc + xa_ext[H - j:H - j + ts, lo:lo + LANES]
        cnt = jnp.minimum(t_glob + 1, win).astype(F32)
        pooled = acc / cnt - cur
        y = _dot(pooled.astype(BF16), pw_ref[g]) * ps_ref[:, lo:lo + LANES]
        mix[:, lo:lo + LANES] = y.astype(BF16)

    CH = 64
    for r0 in range(0, ts, CH):
        acc = jnp.zeros((CH, C), F32) + dwb_ref[...]
        for k in range(CONV_WIDTH):
            off = H - (CONV_WIDTH - 1) + k + r0
            acc = acc + dww_ref[k:k + 1, :] * u_ext[off:off + CH, :]
        mu = jnp.mean(acc, axis=-1, keepdims=True)
        xc = acc - mu
        var = jnp.mean(xc * xc, axis=-1, keepdims=True)
        yn = xc * lax.rsqrt(var + NORM_EPS) * lng_ref[...] + lnb_ref[...]
        mix[r0:r0 + CH, C:2 * C] = _silu(yn).astype(BF16)

    o_ref[0] = x_ref[0] + _dot(mix[...], wo_ref[...])


def _mix_even(xa, u, x, pw, ps, dww, dwb, lng, lnb, wo, ts=512):
    B, S, C = xa.shape
    D = x.shape[-1]
    ts = min(ts, S)
    hb = ts // MIX_HALO
    main = lambda b, i: (b, i, 0)
    halo = lambda b, i: (b, jnp.maximum(i * hb - 1, 0), 0)
    const2 = lambda b, i: (0, 0)
    return pl.pallas_call(
        _mix_even_kernel,
        grid=(B, S // ts),
        in_specs=[
            pl.BlockSpec((1, ts, C), main),
            pl.BlockSpec((1, MIX_HALO, C), halo),
            pl.BlockSpec((1, ts, C), main),
            pl.BlockSpec((1, MIX_HALO, C), halo),
            pl.BlockSpec((1, ts, D), main),
            pl.BlockSpec(pw.shape, lambda b, i: (0, 0, 0)),
            pl.BlockSpec((1, C), const2),
            pl.BlockSpec(dww.shape, const2),
            pl.BlockSpec((1, C), const2),
            pl.BlockSpec((1, C), const2),
            pl.BlockSpec((1, C), const2),
            pl.BlockSpec(wo.shape, const2),
        ],
        out_specs=pl.BlockSpec((1, ts, D), main),
        out_shape=jax.ShapeDtypeStruct((B, S, D), F32),
        scratch_shapes=[
            pltpu.VMEM((MIX_HALO + ts, C), F32),
            pltpu.VMEM((MIX_HALO + ts, C), F32),
            pltpu.VMEM((ts, 2 * C), BF16),
        ],
        compiler_params=_cparams("parallel", "arbitrary"),
        name="mix_even",
    )(xa, xa, u, u, x, pw, ps, dww, dwb, lng, lnb, wo)


def _ffn_kernel(x_ref, g_ref, wg_ref, wu_ref, wd_ref, o_ref, h_scr, act_scr, *, fc):
    x = x_ref[...]
    h_scr[...] = _rmsnorm(x, g_ref[...]).astype(BF16)
    F = wg_ref.shape[1]
    for c0 in range(0, F, fc):
        hb = h_scr[...]
        a = _dot(hb, wg_ref[:, c0:c0 + fc])
        b = _dot(hb, wu_ref[:, c0:c0 + fc])
        act_scr[:, c0:c0 + fc] = (_silu(a) * b).astype(BF16)
    o_ref[...] = x + _dot(act_scr[...], wd_ref[...])


def _ffn(x2, g, wg, wu, wd, tm=512, fc=256):
    T, D = x2.shape
    F = wg.shape[1]
    tm = min(tm, T)
    fc = min(fc, F)
    const = lambda i: (0, 0)
    return pl.pallas_call(
        functools.partial(_ffn_kernel, fc=fc),
        grid=(T // tm,),
        in_specs=[
            pl.BlockSpec((tm, D), lambda i: (i, 0)),
            pl.BlockSpec((1, D), const),
            pl.BlockSpec((D, F), const),
            pl.BlockSpec((D, F), const),
            pl.BlockSpec((F, D), const),
        ],
        out_specs=pl.BlockSpec((tm, D), lambda i: (i, 0)),
        out_shape=jax.ShapeDtypeStruct((T, D), F32),
        scratch_shapes=[pltpu.VMEM((tm, D), BF16), pltpu.VMEM((tm, F), BF16)],
        compiler_params=_cparams("parallel"),
        name="ffn",
    )(x2, g, wg, wu, wd)


def _in_odd_kernel(x_ref, g_ref, w_ref, wf_ref, fb_ref, bd_ref, gq_ref, gk_ref,
                   q_ref, k_ref, v_ref, lf_ref, u_ref, sb_ref):
    A = q_ref.shape[-1]
    hb = _rmsnorm(x_ref[...], g_ref[...]).astype(BF16)

    def headnorm(t, gain):
        ss = _dot((t * t).astype(BF16), bd_ref[...])
        return t * lax.rsqrt(ss * (1.0 / HEAD_DIM) + NORM_EPS) * gain

    q = _dot(hb, w_ref[:, 0:A])
    q_ref[...] = (headnorm(q, gq_ref[...]) * (HEAD_DIM ** -0.5)).astype(BF16)
    k = _dot(hb, w_ref[:, A:2 * A])
    k_ref[...] = headnorm(k, gk_ref[...]).astype(BF16)
    v_ref[...] = _dot(hb, w_ref[:, 2 * A:3 * A]).astype(BF16)
    z = _dot(hb, wf_ref[...]) + fb_ref[...]
    lf_ref[...] = jnp.minimum(z, 0.0) - jnp.log1p(jnp.exp(-jnp.abs(z)))
    sx = _dot(hb, w_ref[:, 3 * A:4 * A])
    sb_ref[...] = _dot(hb, w_ref[:, 4 * A:5 * A]).astype(BF16)
    sc = _dot(hb, w_ref[:, 5 * A:6 * A])
    u_ref[...] = (sc * sx).astype(BF16)


def _in_odd(x2, g, w, wf, fb, bd, gq, gk, tm=512):
    T, D = x2.shape
    A = w.shape[1] // 6
    tm = min(tm, T)
    const = lambda i: (0, 0)
    row = lambda i: (i, 0)
    return pl.pallas_call(
        _in_odd_kernel,
        grid=(T // tm,),
        in_specs=[
            pl.BlockSpec((tm, D), row),
            pl.BlockSpec((1, D), const),
            pl.BlockSpec(w.shape, const),
            pl.BlockSpec(wf.shape, const),
            pl.BlockSpec((1, LANES), const),
            pl.BlockSpec(bd.shape, const),
            pl.BlockSpec((1, A), const),
            pl.BlockSpec((1, A), const),
        ],
        out_specs=[
            pl.BlockSpec((tm, A), row), pl.BlockSpec((tm, A), row), pl.BlockSpec((tm, A), row),
            pl.BlockSpec((tm, LANES), row),
            pl.BlockSpec((tm, A), row), pl.BlockSpec((tm, A), row),
        ],
        out_shape=[
            jax.ShapeDtypeStruct((T, A), BF16), jax.ShapeDtypeStruct((T, A), BF16),
            jax.ShapeDtypeStruct((T, A), BF16), jax.ShapeDtypeStruct((T, LANES), F32),
            jax.ShapeDtypeStruct((T, A), BF16), jax.ShapeDtypeStruct((T, A), BF16),
        ],
        compiler_params=_cparams("parallel"),
        name="in_odd",
    )(x2, g, w, wf, fb, bd, gq, gk)


def _attn_kernel(q_ref, k_ref, v_ref, lf_ref, o_ref, fq_scr, fk_scr, *, tq):
    p = pl.program_id(1)
    qi = pl.program_id(2)
    S = k_ref.shape[1]
    lane = lax.broadcasted_iota(I32, (1, LANES), 1)

    @pl.when(qi == 0)
    def _():
        f = lf_ref[0]
        row = lax.broadcasted_iota(I32, (S, 1), 0)
        sh = 1
        while sh < S:
            f = f + jnp.where(row >= sh, pltpu.roll(f, sh, axis=0), 0.0)
            sh *= 2
        ft = f.T
        sub = lax.broadcasted_iota(I32, (LANES, 1), 0)
        for hl in range(2):
            hg = 2 * p + hl
            col = jnp.sum(jnp.where(lane == hg, f, 0.0), axis=-1, keepdims=True)
            fq_scr[hl] = jnp.broadcast_to(col, (S, LANES))
            fk_scr[hl:hl + 1, :] = jnp.sum(jnp.where(sub == hg, ft, 0.0), axis=0, keepdims=True)

    q0 = pl.multiple_of(qi * tq, tq)
    qb = q_ref[0]
    rowi = lax.broadcasted_iota(I32, (tq, tq), 0)
    coli = lax.broadcasted_iota(I32, (tq, tq), 1)
    causal = rowi >= coli
    nrep = tq // LANES
    outs = []
    for hl in range(2):
        hmask = (lane >= hl * HEAD_DIM) & (lane < (hl + 1) * HEAD_DIM)
        qh = jnp.where(hmask, qb, jnp.zeros_like(qb))
        fq = fq_scr[hl, pl.ds(q0, tq), :]
        fqt = jnp.concatenate([fq] * nrep, axis=1)

        def scores(k0):
            kb = k_ref[0, pl.ds(k0, tq), :]
            s = lax.dot_general(qh, kb, (((1,), (1,)), ((), ())), preferred_element_type=F32)
            return s + fqt - fk_scr[hl:hl + 1, pl.ds(k0, tq)]

        s = jnp.where(causal, scores(q0), -jnp.inf)
        m = jnp.max(s, axis=-1, keepdims=True)
        pr = jnp.exp(s - m)
        l = jnp.sum(pr, axis=-1, keepdims=True)
        acc = _dot(pr.astype(BF16), v_ref[0, pl.ds(q0, tq), :])

        def body(j, carry):
            m, l, acc = carry
            k0 = pl.multiple_of(j * tq, tq)
            s = scores(k0)
            m_new = jnp.maximum(m, jnp.max(s, axis=-1, keepdims=True))
            alpha = jnp.exp(m - m_new)
            pr = jnp.exp(s - m_new)
            l = alpha * l + jnp.sum(pr, axis=-1, keepdims=True)
            acc = alpha * acc + _dot(pr.astype(BF16), v_ref[0, pl.ds(k0, tq), :])
            return m_new, l, acc

        m, l, acc = lax.fori_loop(0, qi, body, (m, l, acc))
        outs.append(acc / l)
    o_ref[0] = jnp.where(lane < HEAD_DIM, outs[0], outs[1]).astype(BF16)


def _attn(q, k, v, lf, tq=512):
    B, S, A = q.shape
    tq = min(tq, S)
    npair = A // LANES
    return pl.pallas_call(
        functools.partial(_attn_kernel, tq=tq),
        grid=(B, npair, S // tq),
        in_specs=[
            pl.BlockSpec((1, tq, LANES), lambda b, p, i: (b, i, p)),
            pl.BlockSpec((1, S, LANES), lambda b, p, i: (b, 0, p)),
            pl.BlockSpec((1, S, LANES), lambda b, p, i: (b, 0, p)),
            pl.BlockSpec((1, S, LANES), lambda b, p, i: (b, 0, 0)),
        ],
        out_specs=pl.BlockSpec((1, tq, LANES), lambda b, p, i: (b, i, p)),
        out_shape=jax.ShapeDtypeStruct((B, S, A), BF16),
        scratch_shapes=[pltpu.VMEM((2, S, LANES), F32), pltpu.VMEM((8, S), F32)],
        compiler_params=_cparams("parallel", "parallel", "arbitrary"),
        name="attn",
    )(q, k, v, lf)


def _out_odd_kernel(yc_ref, u_ref, uh_ref, sb_ref, x_ref, cw_ref, wo_ref, o_ref, u_ext, mix):
    i = pl.program_id(1)
    ts = u_ref.shape[1]
    C = u_ref.shape[2]
    H = SC_HALO

    @pl.when(i == 0)
    def _():
        u_ext[0:H, :] = jnp.zeros((H, C), F32)

    @pl.when(i > 0)
    def _():
        u_ext[0:H, :] = uh_ref[0].astype(F32)

    u_ext[H:H + ts, :] = u_ref[0].astype(F32)
    conv = jnp.zeros((ts, C), F32)
    for k in range(SC_WIDTH):
        off = H - (SC_WIDTH - 1) + k
        conv = conv + cw_ref[k:k + 1, :] * u_ext[off:off + ts, :]
    mix[:, 0:C] = yc_ref[0]
    mix[:, C:2 * C] = (sb_ref[0].astype(F32) * conv).astype(BF16)
    o_ref[0] = x_ref[0] + _dot(mix[...], wo_ref[...])


def _out_odd(yc, u, sb, x, cw, wo, ts=512):
    B, S, C = u.shape
    D = x.shape[-1]
    ts = min(ts, S)
    hb = ts // SC_HALO
    main = lambda b, i: (b, i, 0)
    halo = lambda b, i: (b, jnp.maximum(i * hb - 1, 0), 0)
    const2 = lambda b, i: (0, 0)
    return pl.pallas_call(
        _out_odd_kernel,
        grid=(B, S // ts),
        in_specs=[
            pl.BlockSpec((1, ts, C), main),
            pl.BlockSpec((1, ts, C), main),
            pl.BlockSpec((1, SC_HALO, C), halo),
            pl.BlockSpec((1, ts, C), main),
            pl.BlockSpec((1, ts, D), main),
            pl.BlockSpec(cw.shape, const2),
            pl.BlockSpec(wo.shape, const2),
        ],
        out_specs=pl.BlockSpec((1, ts, D), main),
        out_shape=jax.ShapeDtypeStruct((B, S, D), F32),
        scratch_shapes=[pltpu.VMEM((SC_HALO + ts, C), F32), pltpu.VMEM((ts, 2 * C), BF16)],
        compiler_params=_cparams("parallel", "arbitrary"),
        name="out_odd",
    )(yc, u, u, sb, x, cw, wo)


def _router_kernel(x_ref, g_ref, rw_ref, rb_ref, meta_ref, cnt_ref, base_scr, *, n_exp):
    i = pl.program_id(0)
    tm = x_ref.shape[0]

    @pl.when(i == 0)
    def _():
        base_scr[...] = jnp.zeros_like(base_scr)

    h = _rmsnorm(x_ref[...], g_ref[...])
    lane = lax.broadcasted_iota(I32, (1, LANES), 1)
    logits = jnp.full((tm, LANES), -jnp.inf, F32)
    for e in range(n_exp):
        s = jnp.sum(h * rw_ref[e:e + 1, :], axis=-1, keepdims=True)
        logits = jnp.where(lane == e, s, logits)
    logits = logits + rb_ref[...]

    m1 = jnp.max(logits, axis=-1, keepdims=True)
    i1 = jnp.min(jnp.where(logits == m1, lane, LANES), axis=-1, keepdims=True)
    rest = jnp.where(lane == i1, -jnp.inf, logits)
    m2 = jnp.max(rest, axis=-1, keepdims=True)
    i2 = jnp.min(jnp.where(rest == m2, lane, LANES), axis=-1, keepdims=True)
    e2 = jnp.exp(m2 - m1)
    g1 = 1.0 / (1.0 + e2)
    g2 = e2 / (1.0 + e2)

    oh1 = (lane == i1)
    oh2 = (lane == i2)
    sel = jnp.where(oh1 | oh2, 1.0, 0.0).astype(BF16)
    r = lax.broadcasted_iota(I32, (tm, tm), 0)
    c = lax.broadcasted_iota(I32, (tm, tm), 1)
    strict = jnp.where(r > c, 1.0, 0.0).astype(BF16)
    before = _dot(strict, sel) + base_scr[...]
    r1 = jnp.sum(jnp.where(oh1, before, 0.0), axis=-1, keepdims=True)
    r2 = jnp.sum(jnp.where(oh2, before, 0.0), axis=-1, keepdims=True)
    base_scr[...] = base_scr[...] + jnp.sum(sel.astype(F32), axis=0, keepdims=True)

    meta = jnp.zeros((tm, LANES), F32)
    for idx, val in enumerate((i1.astype(F32), i2.astype(F32), g1, g2, r1, r2)):
        meta = jnp.where(lane == idx, val, meta)
    meta_ref[...] = meta
    cnt_ref[...] = base_scr[...]


def _router(x2, g, rwt, rb, n_exp, tm=512):
    T, D = x2.shape
    tm = min(tm, T)
    const = lambda i: (0, 0)
    return pl.pallas_call(
        functools.partial(_router_kernel, n_exp=n_exp),
        grid=(T // tm,),
        in_specs=[
            pl.BlockSpec((tm, D), lambda i: (i, 0)),
            pl.BlockSpec((1, D), const),
            pl.BlockSpec(rwt.shape, const),
            pl.BlockSpec((1, LANES), const),
        ],
        out_specs=[pl.BlockSpec((tm, LANES), lambda i: (i, 0)), pl.BlockSpec((1, LANES), const)],
        out_shape=[jax.ShapeDtypeStruct((T, LANES), F32), jax.ShapeDtypeStruct((1, LANES), F32)],
        scratch_shapes=[pltpu.VMEM((1, LANES), F32)],
        compiler_params=_cparams("arbitrary"),
        name="router",
    )(x2, g, rwt, rb)


def _dispatch_kernel(pos_ref, pad_ref, x_ref, xs_hbm, zbuf, sem, zsem, *, n_exp, tile):
    i = pl.program_id(0)
    tc = x_ref.shape[0]
    T = pos_ref.shape[0] // TOP_K

    def zero_copies(e):
        start, head, rem = pad_ref[3 * e], pad_ref[3 * e + 1], pad_ref[3 * e + 2]
        out = []
        for r in range(7):
            out.append((r < head, pltpu.make_async_copy(
                zbuf.at[pl.ds(0, 1)], xs_hbm.at[pl.ds(start + r, 1)], zsem)))
        for b in range(3, (tile - 1).bit_length()):
            n = 1 << b
            dst = pl.multiple_of(start + head + (rem & (n - 1)), 8)
            out.append((((rem >> b) & 1) == 1, pltpu.make_async_copy(
                zbuf.at[pl.ds(0, n)], xs_hbm.at[pl.ds(dst, n)], zsem)))
        return out

    def tail_copies():
        zr = zbuf.shape[0]
        n_active = pad_ref[3 * n_exp]
        out = []
        for t in range(xs_hbm.shape[0] // tile):
            for r0 in range(0, tile, zr):
                out.append((t >= n_active, pltpu.make_async_copy(
                    zbuf, xs_hbm.at[pl.ds(t * tile + r0, zr)], zsem)))
        return out

    def all_zero_copies():
        return [pc for e in range(n_exp) for pc in zero_copies(e)] + tail_copies()

    @pl.when(i == 0)
    def _():
        zbuf[...] = jnp.zeros_like(zbuf)
        for pred, cp in all_zero_copies():
            @pl.when(pred)
            def _():
                cp.start()

    base = i * tc

    def tok(r, _):
        for k in range(TOP_K):
            slot = pos_ref[k * T + base + r]
            pltpu.make_async_copy(x_ref.at[pl.ds(r, 1)], xs_hbm.at[pl.ds(slot, 1)], sem).start()
        return 0

    lax.fori_loop(0, tc, tok, 0, unroll=8)
    for k in range(TOP_K):
        pltpu.make_async_copy(x_ref, xs_hbm.at[pl.ds(0, tc)], sem).wait()

    @pl.when(i == pl.num_programs(0) - 1)
    def _():
        for pred, cp in all_zero_copies():
            @pl.when(pred)
            def _():
                cp.wait()


def _dispatch(pos, pad, x2, n_rows, n_exp, tile, tc=512):
    T, D = x2.shape
    tc = min(tc, T)
    return pl.pallas_call(
        functools.partial(_dispatch_kernel, n_exp=n_exp, tile=tile),
        grid_spec=pltpu.PrefetchScalarGridSpec(
            num_scalar_prefetch=2,
            grid=(T // tc,),
            in_specs=[pl.BlockSpec((tc, D), lambda i, pos, pad: (i, 0))],
            out_specs=pl.BlockSpec(memory_space=pl.ANY),
            scratch_shapes=[
                pltpu.VMEM((max(tile // 2, 8), D), F32),
                pltpu.SemaphoreType.DMA,
                pltpu.SemaphoreType.DMA,
            ],
        ),
        out_shape=jax.ShapeDtypeStruct((n_rows, D), F32),
        compiler_params=_cparams("arbitrary"),
        name="dispatch",
    )(pos, pad, x2)


def _experts_kernel(te_ref, na_ref, xs_ref, g_ref, wg_ref, wu_ref, wd_ref, y_ref, h_scr, acc_scr):
    t = pl.program_id(0)
    j = pl.program_id(1)
    nj = pl.num_programs(1)

    @pl.when(t < na_ref[0])
    def _():
        @pl.when(j == 0)
        def _():
            h_scr[...] = _rmsnorm(xs_ref[...], g_ref[...]).astype(BF16)
            acc_scr[...] = jnp.zeros_like(acc_scr)

        hb = h_scr[...]
        a = _dot(hb, wg_ref[0, 0].astype(BF16))
        b = _dot(hb, wu_ref[0, 0].astype(BF16))
        act = (_silu(a) * b).astype(BF16)
        acc_scr[...] += _dot(act, wd_ref[0, 0].astype(BF16))

        @pl.when(j == nj - 1)
        def _():
            y_ref[...] = acc_scr[...]

    @pl.when((t >= na_ref[0]) & (j == 0))
    def _():
        y_ref[...] = jnp.zeros_like(y_ref)


def _experts(tile_exp, n_active, xs, g, wg, wu, wd, layer, tile, tf):
    P, D = xs.shape
    F = wg.shape[-1]
    tf = min(tf, F)
    nj = F // tf
    n_tiles = P // tile

    def row_map(t, j, te, na):
        return (jnp.minimum(t, na[0] - 1), 0)

    def jeff(t, j, na):
        return jnp.where(t < na[0], j, nj - 1)

    return pl.pallas_call(
        _experts_kernel,
        grid_spec=pltpu.PrefetchScalarGridSpec(
            num_scalar_prefetch=2,
            grid=(n_tiles, nj),
            in_specs=[
                pl.BlockSpec((tile, D), row_map),
                pl.BlockSpec((1, D), lambda t, j, te, na: (0, 0)),
                pl.BlockSpec((1, 1, D, tf), lambda t, j, te, na: (layer, te[t], 0, jeff(t, j, na))),
                pl.BlockSpec((1, 1, D, tf), lambda t, j, te, na: (layer, te[t], 0, jeff(t, j, na))),
                pl.BlockSpec((1, 1, tf, D), lambda t, j, te, na: (layer, te[t], jeff(t, j, na), 0)),
            ],
            out_specs=pl.BlockSpec((tile, D), lambda t, j, te, na: (t, 0)),
            scratch_shapes=[pltpu.VMEM((tile, D), BF16), pltpu.VMEM((tile, D), F32)],
        ),
        out_shape=jax.ShapeDtypeStruct((P, D), F32),
        compiler_params=_cparams("arbitrary", "arbitrary"),
        name="experts",
    )(tile_exp, n_active, xs, g, wg, wu, wd)


def _combine_kernel(pos_ref, x_ref, meta_ref, y_hbm, o_ref, buf, sem):
    i = pl.program_id(0)
    tc = x_ref.shape[0]
    T = pos_ref.shape[0] // TOP_K
    base = i * tc

    def issue(r, _):
        for k in range(TOP_K):
            slot = pos_ref[k * T + base + r]
            pltpu.make_async_copy(y_hbm.at[pl.ds(slot, 1)], buf.at[k, pl.ds(r, 1)], sem).start()
        return 0

    lax.fori_loop(0, tc, issue, 0, unroll=8)
    for k in range(TOP_K):
        pltpu.make_async_copy(y_hbm.at[pl.ds(0, tc)], buf.at[k], sem).wait()
    meta = meta_ref[...]
    o_ref[...] = x_ref[...] + meta[:, 2:3] * buf[0] + meta[:, 3:4] * buf[1]


def _combine(pos, x2, meta, y, tc=512):
    T, D = x2.shape
    tc = min(tc, T)
    return pl.pallas_call(
        _combine_kernel,
        grid_spec=pltpu.PrefetchScalarGridSpec(
            num_scalar_prefetch=1,
            grid=(T // tc,),
            in_specs=[
                pl.BlockSpec((tc, D), lambda i, pos: (i, 0)),
                pl.BlockSpec((tc, LANES), lambda i, pos: (i, 0)),
                pl.BlockSpec(memory_space=pl.ANY),
            ],
            out_specs=pl.BlockSpec((tc, D), lambda i, pos: (i, 0)),
            scratch_shapes=[pltpu.VMEM((TOP_K, tc, D), F32), pltpu.SemaphoreType.DMA],
        ),
        out_shape=jax.ShapeDtypeStruct((T, D), F32),
        compiler_params=_cparams("arbitrary"),
        name="combine",
    )(pos, x2, meta, y)


def _moe(x2, g, rw, rb, wg, wu, wd, layer):
    T, D = x2.shape
    E = rw.shape[1]
    tile = min(MOE_TILE, T)
    n_tiles = (T * TOP_K) // tile + E
    rwt = jnp.zeros((8 * ((E + 7) // 8), D), F32).at[:E].set(rw.T)
    rbp = jnp.zeros((1, LANES), F32).at[0, :E].set(rb)
    meta, cnt = _router(x2, g, rwt, rbp, E)

    counts = cnt[0, :E].astype(I32)
    tiles_e = (counts + tile - 1) // tile
    tile_end = jnp.cumsum(tiles_e)
    row_off = (tile_end - tiles_e) * tile
    n_active = tile_end[-1:]
    tile_exp = jnp.minimum(jnp.searchsorted(tile_end, jnp.arange(n_tiles, dtype=I32), side="right"), E - 1)
    tile_exp = jnp.where(jnp.arange(n_tiles) < n_active[0], tile_exp, tile_exp[jnp.maximum(n_active[0] - 1, 0)])
    pad_start = row_off + counts
    pad_head = (-pad_start) & 7
    pad_rest = tiles_e * tile - counts - pad_head
    pad = jnp.concatenate([jnp.stack([pad_start, pad_head, pad_rest], axis=1).reshape(-1), n_active]).astype(I32)
    eid = meta[:, 0:TOP_K].astype(I32)
    rank = meta[:, 4:4 + TOP_K].astype(I32)
    pos = (row_off[eid] + rank).T.reshape(-1)

    xs = _dispatch(pos, pad, x2, n_tiles * tile, E, tile)
    y = _experts(tile_exp.astype(I32), n_active.astype(I32), xs, g, wg, wu, wd, layer, tile, MOE_FF_TILE)
    return _combine(pos, x2, meta, y)


def kernel(x, ev_norm_mix, ev_w_in, pool_w, pool_scale, conv_dw_w, conv_dw_b, conv_ln_g, conv_ln_b, ev_w_out, ev_norm_ffn, ffn_w_gate, ffn_w_up, ffn_w_down, od_norm_mix, od_w_in, forget_bias, q_norm_g, k_norm_g, sc_conv_w, od_w_out, od_norm_ffn, router_w, router_b, moe_w_gate, moe_w_up, moe_w_down):
    B, S, D = x.shape
    T = B * S
    depth = ev_norm_mix.shape[0] + od_norm_mix.shape[0]
    A = od_w_in.shape[2] * HEAD_DIM // (6 * HEAD_DIM + 1)
    H = A // HEAD_DIM
    row = lambda v: v.reshape(1, -1)
    head = jnp.arange(A) // HEAD_DIM
    bd = (head[:, None] == head[None, :]).astype(BF16)

    x2 = x.reshape(T, D)
    for layer in range(depth):
        i = layer // 2
        if layer % 2 == 0:
            C = pool_scale.shape[1]
            xa, u = _in_even(x2, row(ev_norm_mix[i]), ev_w_in[i].astype(BF16))
            x3 = _mix_even(
                xa.reshape(B, S, C), u.reshape(B, S, C), x2.reshape(B, S, D),
                pool_w[i].astype(BF16), row(pool_scale[i]), conv_dw_w[i], row(conv_dw_b[i]),
                row(conv_ln_g[i]), row(conv_ln_b[i]), ev_w_out[i].astype(BF16))
            x2 = _ffn(x3.reshape(T, D), row(ev_norm_ffn[i]), ffn_w_gate[i].astype(BF16),
                      ffn_w_up[i].astype(BF16), ffn_w_down[i].astype(BF16))
        else:
            w = od_w_in[i]
            w_main = jnp.concatenate([w[:, :3 * A], w[:, 3 * A + H:]], axis=1).astype(BF16)
            w_f = jnp.zeros((D, LANES), F32).at[:, :H].set(w[:, 3 * A:3 * A + H]).astype(BF16)
            fb = jnp.zeros((1, LANES), F32).at[0, :H].set(forget_bias[i])
            q, k, v, lf, u, sb = _in_odd(
                x2, row(od_norm_mix[i]), w_main, w_f, fb, bd,
                row(jnp.tile(q_norm_g[i], H)), row(jnp.tile(k_norm_g[i], H)))
            sh = lambda t: t.reshape(B, S, -1)
            yc = _attn(sh(q), sh(k), sh(v), sh(lf))
            x3 = _out_odd(yc, sh(u), sh(sb), x2.reshape(B, S, D), sc_conv_w[i], od_w_out[i].astype(BF16))
            x2 = _moe(x3.reshape(T, D), row(od_norm_ffn[i]), router_w[i], router_b[i],
                      moe_w_gate, moe_w_up, moe_w_down, i)
    return x2.reshape(B, S, D)
```

```python
import functools

import jax
import jax.numpy as jnp
from jax import lax
from jax.experimental import pallas as pl
from jax.experimental.pallas import tpu as pltpu

F32 = jnp.float32
BF16 = jnp.bfloat16
I32 = jnp.int32

NORM_EPS = 1e-6
LANES = 128
POOL_WINDOWS = (2, 4, 8, 16)
CONV_WIDTH = 31
SC_WIDTH = 3
HEAD_DIM = 64
TOP_K = 2
LOG2E = 1.4426950408889634
MIX_HALO = 32
SC_HALO = 16
MOE_TILE = 1024
MOE_FF_TILE = 512
VMEM_LIMIT = 56 * 1024 * 1024


def _cparams(*sem):
    return pltpu.CompilerParams(dimension_semantics=sem, vmem_limit_bytes=VMEM_LIMIT)


def _rmsnorm(x, g):
    return x * lax.rsqrt(jnp.mean(x * x, axis=-1, keepdims=True) + NORM_EPS) * g


def _sigmoid(x):
    return 1.0 / (1.0 + jnp.exp(-x))


def _silu(x):
    return x * _sigmoid(x)


def _dot(a, b):
    return jnp.dot(a, b, preferred_element_type=F32)


def _in_even_kernel(x_ref, g_ref, w_ref, xa_ref, u_ref):
    c = xa_ref.shape[-1]
    hb = _rmsnorm(x_ref[...], g_ref[...]).astype(BF16)
    xa_ref[...] = _dot(hb, w_ref[:, 0:c]).astype(BF16)
    val = _dot(hb, w_ref[:, c:2 * c])
    gate = _dot(hb, w_ref[:, 2 * c:3 * c])
    u_ref[...] = (val * _sigmoid(gate)).astype(BF16)


def _in_even(x2, g, w, tm=512):
    T, D = x2.shape
    C = w.shape[1] // 3
    tm = min(tm, T)
    return pl.pallas_call(
        _in_even_kernel,
        grid=(T // tm,),
        in_specs=[
            pl.BlockSpec((tm, D), lambda i: (i, 0)),
            pl.BlockSpec((1, D), lambda i: (0, 0)),
            pl.BlockSpec((D, 3 * C), lambda i: (0, 0)),
        ],
        out_specs=[
            pl.BlockSpec((tm, C), lambda i: (i, 0)),
            pl.BlockSpec((tm, C), lambda i: (i, 0)),
        ],
        out_shape=[jax.ShapeDtypeStruct((T, C), BF16)] * 2,
        compiler_params=_cparams("parallel"),
        name="in_even",
    )(x2, g, w)


def _mix_even_kernel(xa_ref, xah_ref, u_ref, uh_ref, x_ref, pw_ref, ps_ref, dww_ref,
                     dwb_ref, lng_ref, lnb_ref, wo_ref, o_ref, xa_ext, u_ext, mix, shf):
    i = pl.program_id(1)
    ts = xa_ref.shape[1]
    C = xa_ref.shape[2]
    H = MIX_HALO

    @pl.when(i == 0)
    def _():
        xa_ext[0:H, :] = jnp.zeros((H, C), F32)
        u_ext[0:H, :] = jnp.zeros((H, C), F32)

    @pl.when(i > 0)
    def _():
        xa_ext[0:H, :] = xah_ref[0].astype(F32)
        u_ext[0:H, :] = uh_ref[0].astype(F32)

    xa_ext[H:H + ts, :] = xa_ref[0].astype(F32)
    u_ext[H:H + ts, :] = u_ref[0].astype(F32)

    t_glob = i * ts + lax.broadcasted_iota(I32, (ts, 1), 0)
    for g, win in enumerate(POOL_WINDOWS):
        lo = g * LANES
        acc = xa_ext[H:H + ts, lo:lo + LANES]
        cur = acc
        for j in range(1, win):
            acc = acc + xa_ext[H - j:H - j + ts, lo:lo + LANES]
        cnt = jnp.minimum(t_glob + 1, win).astype(F32)
        pooled = acc / cnt - cur
        y = _dot(pooled.astype(BF16), pw_ref[g]) * ps_ref[:, lo:lo + LANES]
        mix[:, lo:lo + LANES] = y.astype(BF16)

    CH = 64
    SUB = 8
    n_a = (CONV_WIDTH - 1) // SUB + 1
    back = SUB * (n_a - 1)
    for r0 in range(0, ts, CH):
        acc = jnp.zeros((CH, C), F32) + dwb_ref[...]
        for b in range(SUB):
            lo = H - back - b + r0
            shf[...] = u_ext[lo:lo + CH + back, :]
            for a in range(n_a):
                shift = SUB * a + b
                if shift < CONV_WIDTH:
                    k = CONV_WIDTH - 1 - shift
                    acc = acc + dww_ref[k:k + 1, :] * shf[back - SUB * a:back - SUB * a + CH, :]
        mu = jnp.mean(acc, axis=-1, keepdims=True)
        xc = acc - mu
        var = jnp.mean(xc * xc, axis=-1, keepdims=True)
        yn = xc * lax.rsqrt(var + NORM_EPS) * lng_ref[...] + lnb_ref[...]
        mix[r0:r0 + CH, C:2 * C] = _silu(yn).astype(BF16)

    o_ref[0] = x_ref[0] + _dot(mix[...], wo_ref[...])


def _mix_even(xa, u, x, pw, ps, dww, dwb, lng, lnb, wo, ts=512):
    B, S, C = xa.shape
    D = x.shape[-1]
    ts = min(ts, S)
    hb = ts // MIX_HALO
    main = lambda b, i: (b, i, 0)
    halo = lambda b, i: (b, jnp.maximum(i * hb - 1, 0), 0)
    const2 = lambda b, i: (0, 0)
    return pl.pallas_call(
        _mix_even_kernel,
        grid=(B, S // ts),
        in_specs=[
            pl.BlockSpec((1, ts, C), main),
            pl.BlockSpec((1, MIX_HALO, C), halo),
            pl.BlockSpec((1, ts, C), main),
            pl.BlockSpec((1, MIX_HALO, C), halo),
            pl.BlockSpec((1, ts, D), main),
            pl.BlockSpec(pw.shape, lambda b, i: (0, 0, 0)),
            pl.BlockSpec((1, C), const2),
            pl.BlockSpec(dww.shape, const2),
            pl.BlockSpec((1, C), const2),
            pl.BlockSpec((1, C), const2),
            pl.BlockSpec((1, C), const2),
            pl.BlockSpec(wo.shape, const2),
        ],
        out_specs=pl.BlockSpec((1, ts, D), main),
        out_shape=jax.ShapeDtypeStruct((B, S, D), F32),
        scratch_shapes=[
            pltpu.VMEM((MIX_HALO + ts, C), F32),
            pltpu.VMEM((MIX_HALO + ts, C), F32),
            pltpu.VMEM((ts, 2 * C), BF16),
            pltpu.VMEM((64 + 8 * ((CONV_WIDTH - 1) // 8), C), F32),
        ],
        compiler_params=_cparams("parallel", "arbitrary"),
        name="mix_even",
    )(xa, xa, u, u, x, pw, ps, dww, dwb, lng, lnb, wo)


def _ffn_kernel(x_ref, g_ref, wg_ref, wu_ref, wd_ref, o_ref, h_scr, act_scr, *, fc):
    x = x_ref[...]
    h_scr[...] = _rmsnorm(x, g_ref[...]).astype(BF16)
    F = wg_ref.shape[1]
    for c0 in range(0, F, fc):
        hb = h_scr[...]
        a = _dot(hb, wg_ref[:, c0:c0 + fc])
        b = _dot(hb, wu_ref[:, c0:c0 + fc])
        act_scr[:, c0:c0 + fc] = (_silu(a) * b).astype(BF16)
    o_ref[...] = x + _dot(act_scr[...], wd_ref[...])


def _ffn(x2, g, wg, wu, wd, tm=512, fc=256):
    T, D = x2.shape
    F = wg.shape[1]
    tm = min(tm, T)
    fc = min(fc, F)
    const = lambda i: (0, 0)
    return pl.pallas_call(
        functools.partial(_ffn_kernel, fc=fc),
        grid=(T // tm,),
        in_specs=[
            pl.BlockSpec((tm, D), lambda i: (i, 0)),
            pl.BlockSpec((1, D), const),
            pl.BlockSpec((D, F), const),
            pl.BlockSpec((D, F), const),
            pl.BlockSpec((F, D), const),
        ],
        out_specs=pl.BlockSpec((tm, D), lambda i: (i, 0)),
        out_shape=jax.ShapeDtypeStruct((T, D), F32),
        scratch_shapes=[pltpu.VMEM((tm, D), BF16), pltpu.VMEM((tm, F), BF16)],
        compiler_params=_cparams("parallel"),
        name="ffn",
    )(x2, g, wg, wu, wd)


def _in_odd_kernel(x_ref, g_ref, w_ref, wf_ref, fb_ref, bd_ref, gq_ref, gk_ref,
                   q_ref, k_ref, v_ref, lf_ref, u_ref, sb_ref):
    A = q_ref.shape[-1]
    hb = _rmsnorm(x_ref[...], g_ref[...]).astype(BF16)

    def headnorm(t, gain):
        ss = _dot((t * t).astype(BF16), bd_ref[...])
        return t * lax.rsqrt(ss * (1.0 / HEAD_DIM) + NORM_EPS) * gain

    q = _dot(hb, w_ref[:, 0:A])
    q_ref[...] = (headnorm(q, gq_ref[...]) * (HEAD_DIM ** -0.5 * LOG2E)).astype(BF16)
    k = _dot(hb, w_ref[:, A:2 * A])
    k_ref[...] = headnorm(k, gk_ref[...]).astype(BF16)
    v_ref[...] = _dot(hb, w_ref[:, 2 * A:3 * A]).astype(BF16)
    z = _dot(hb, wf_ref[...]) + fb_ref[...]
    lf_ref[...] = (jnp.minimum(z, 0.0) - jnp.log1p(jnp.exp(-jnp.abs(z)))) * LOG2E
    sx = _dot(hb, w_ref[:, 3 * A:4 * A])
    sb_ref[...] = _dot(hb, w_ref[:, 4 * A:5 * A]).astype(BF16)
    sc = _dot(hb, w_ref[:, 5 * A:6 * A])
    u_ref[...] = (sc * sx).astype(BF16)


def _in_odd(x2, g, w, wf, fb, bd, gq, gk, tm=512):
    T, D = x2.shape
    A = w.shape[1] // 6
    tm = min(tm, T)
    const = lambda i: (0, 0)
    row = lambda i: (i, 0)
    return pl.pallas_call(
        _in_odd_kernel,
        grid=(T // tm,),
        in_specs=[
            pl.BlockSpec((tm, D), row),
            pl.BlockSpec((1, D), const),
            pl.BlockSpec(w.shape, const),
            pl.BlockSpec(wf.shape, const),
            pl.BlockSpec((1, LANES), const),
            pl.BlockSpec(bd.shape, const),
            pl.BlockSpec((1, A), const),
            pl.BlockSpec((1, A), const),
        ],
        out_specs=[
            pl.BlockSpec((tm, A), row), pl.BlockSpec((tm, A), row), pl.BlockSpec((tm, A), row),
            pl.BlockSpec((tm, LANES), row),
            pl.BlockSpec((tm, A), row), pl.BlockSpec((tm, A), row),
        ],
        out_shape=[
            jax.ShapeDtypeStruct((T, A), BF16), jax.ShapeDtypeStruct((T, A), BF16),
            jax.ShapeDtypeStruct((T, A), BF16), jax.ShapeDtypeStruct((T, LANES), F32),
            jax.ShapeDtypeStruct((T, A), BF16), jax.ShapeDtypeStruct((T, A), BF16),
        ],
        compiler_params=_cparams("parallel"),
        name="in_odd",
    )(x2, g, w, wf, fb, bd, gq, gk)


def _attn_kernel(q_ref, k_ref, v_ref, lf_ref, o_ref, qa_scr, ka_scr, *, tq):
    p = pl.program_id(1)
    qi = pl.program_id(2)
    S = k_ref.shape[1]
    lane = lax.broadcasted_iota(I32, (1, LANES), 1)

    def head_lanes(x, hl):
        return x if hl == 0 else pltpu.roll(x, HEAD_DIM, axis=1)

    @pl.when(qi == 0)
    def _():
        f = lf_ref[0]
        row = lax.broadcasted_iota(I32, (S, 1), 0)
        sh = 1
        while sh < S:
            f = f + jnp.where(row >= sh, pltpu.roll(f, sh, axis=0), 0.0)
            sh *= 2
        kf = k_ref[0].astype(F32)
        for hl in range(2):
            hg = 2 * p + hl
            col = jnp.sum(jnp.where(lane == hg, f, 0.0), axis=-1, keepdims=True)
            parts = []
            rem = col
            for _ in range(3):
                part = rem.astype(BF16).astype(F32)
                parts.append(part)
                rem = rem - part
            qa = jnp.zeros((S, LANES), F32)
            ka = head_lanes(kf, hl)
            for n, part in enumerate(parts):
                qa = jnp.where(lane == HEAD_DIM + n, part, qa)
                qa = jnp.where(lane == HEAD_DIM + 3 + n, 1.0, qa)
                ka = jnp.where(lane == HEAD_DIM + n, 1.0, ka)
                ka = jnp.where(lane == HEAD_DIM + 3 + n, -part, ka)
            ka = jnp.where(lane >= HEAD_DIM + 6, 0.0, ka)
            qa_scr[hl] = qa.astype(BF16)
            ka_scr[hl] = ka.astype(BF16)

    q0 = pl.multiple_of(qi * tq, tq)
    qf = q_ref[0].astype(F32)
    rowi = lax.broadcasted_iota(I32, (tq, tq), 0)
    coli = lax.broadcasted_iota(I32, (tq, tq), 1)
    causal = rowi >= coli
    qh = [jnp.where(lane < HEAD_DIM, head_lanes(qf, hl),
                    qa_scr[hl, pl.ds(q0, tq), :].astype(F32)).astype(BF16) for hl in range(2)]

    def scores(hl, k0):
        kb = ka_scr[hl, pl.ds(k0, tq), :]
        return lax.dot_general(qh[hl], kb, (((1,), (1,)), ((), ())), preferred_element_type=F32)

    carry = []
    for hl in range(2):
        s = jnp.where(causal, scores(hl, q0), -jnp.inf)
        m = jnp.max(s, axis=-1, keepdims=True)
        pr = jnp.exp2(s - m)
        l = jnp.sum(pr, axis=-1, keepdims=True)
        carry += [m, l, _dot(pr.astype(BF16), v_ref[0, pl.ds(q0, tq), :])]

    def body(j, carry):
        k0 = pl.multiple_of(j * tq, tq)
        vb = v_ref[0, pl.ds(k0, tq), :]
        out = []
        for hl in range(2):
            m, l, acc = carry[3 * hl:3 * hl + 3]
            s = scores(hl, k0)
            m_new = jnp.maximum(m, jnp.max(s, axis=-1, keepdims=True))
            alpha = jnp.exp2(m - m_new)
            pr = jnp.exp2(s - m_new)
            l = alpha * l + jnp.sum(pr, axis=-1, keepdims=True)
            out += [m_new, l, alpha * acc + _dot(pr.astype(BF16), vb)]
        return tuple(out)

    carry = lax.fori_loop(0, qi, body, tuple(carry))
    outs = [carry[3 * hl + 2] / carry[3 * hl + 1] for hl in range(2)]
    o_ref[0] = jnp.where(lane < HEAD_DIM, outs[0], outs[1]).astype(BF16)


def _attn(q, k, v, lf, tq=512):
    B, S, A = q.shape
    tq = min(tq, S)
    npair = A // LANES
    return pl.pallas_call(
        functools.partial(_attn_kernel, tq=tq),
        grid=(B, npair, S // tq),
        in_specs=[
            pl.BlockSpec((1, tq, LANES), lambda b, p, i: (b, i, p)),
            pl.BlockSpec((1, S, LANES), lambda b, p, i: (b, 0, p)),
            pl.BlockSpec((1, S, LANES), lambda b, p, i: (b, 0, p)),
            pl.BlockSpec((1, S, LANES), lambda b, p, i: (b, 0, 0)),
        ],
        out_specs=pl.BlockSpec((1, tq, LANES), lambda b, p, i: (b, i, p)),
        out_shape=jax.ShapeDtypeStruct((B, S, A), BF16),
        scratch_shapes=[pltpu.VMEM((2, S, LANES), BF16), pltpu.VMEM((2, S, LANES), BF16)],
        compiler_params=_cparams("parallel", "parallel", "arbitrary"),
        name="attn",
    )(q, k, v, lf)


def _out_odd_kernel(yc_ref, u_ref, uh_ref, sb_ref, x_ref, cw_ref, wo_ref, o_ref, u_ext, mix):
    i = pl.program_id(1)
    ts = u_ref.shape[1]
    C = u_ref.shape[2]
    H = SC_HALO

    @pl.when(i == 0)
    def _():
        u_ext[0:H, :] = jnp.zeros((H, C), F32)

    @pl.when(i > 0)
    def _():
        u_ext[0:H, :] = uh_ref[0].astype(F32)

    u_ext[H:H + ts, :] = u_ref[0].astype(F32)
    conv = jnp.zeros((ts, C), F32)
    for k in range(SC_WIDTH):
        off = H - (SC_WIDTH - 1) + k
        conv = conv + cw_ref[k:k + 1, :] * u_ext[off:off + ts, :]
    mix[:, 0:C] = yc_ref[0]
    mix[:, C:2 * C] = (sb_ref[0].astype(F32) * conv).astype(BF16)
    o_ref[0] = x_ref[0] + _dot(mix[...], wo_ref[...])


def _out_odd(yc, u, sb, x, cw, wo, ts=512):
    B, S, C = u.shape
    D = x.shape[-1]
    ts = min(ts, S)
    hb = ts // SC_HALO
    main = lambda b, i: (b, i, 0)
    halo = lambda b, i: (b, jnp.maximum(i * hb - 1, 0), 0)
    const2 = lambda b, i: (0, 0)
    return pl.pallas_call(
        _out_odd_kernel,
        grid=(B, S // ts),
        in_specs=[
            pl.BlockSpec((1, ts, C), main),
            pl.BlockSpec((1, ts, C), main),
            pl.BlockSpec((1, SC_HALO, C), halo),
            pl.BlockSpec((1, ts, C), main),
            pl.BlockSpec((1, ts, D), main),
            pl.BlockSpec(cw.shape, const2),
            pl.BlockSpec(wo.shape, const2),
        ],
        out_specs=pl.BlockSpec((1, ts, D), main),
        out_shape=jax.ShapeDtypeStruct((B, S, D), F32),
        scratch_shapes=[pltpu.VMEM((SC_HALO + ts, C), F32), pltpu.VMEM((ts, 2 * C), BF16)],
        compiler_params=_cparams("parallel", "arbitrary"),
        name="out_odd",
    )(yc, u, u, sb, x, cw, wo)


def _router_kernel(x_ref, g_ref, rw_ref, rb_ref, meta_ref, cnt_ref, base_scr, *, n_exp):
    i = pl.program_id(0)
    tm = x_ref.shape[0]

    @pl.when(i == 0)
    def _():
        base_scr[...] = jnp.zeros_like(base_scr)

    h = _rmsnorm(x_ref[...], g_ref[...])
    lane = lax.broadcasted_iota(I32, (1, LANES), 1)
    logits = jnp.full((tm, LANES), -jnp.inf, F32)
    for e in range(n_exp):
        s = jnp.sum(h * rw_ref[e:e + 1, :], axis=-1, keepdims=True)
        logits = jnp.where(lane == e, s, logits)
    logits = logits + rb_ref[...]

    m1 = jnp.max(logits, axis=-1, keepdims=True)
    i1 = jnp.min(jnp.where(logits == m1, lane, LANES), axis=-1, keepdims=True)
    rest = jnp.where(lane == i1, -jnp.inf, logits)
    m2 = jnp.max(rest, axis=-1, keepdims=True)
    i2 = jnp.min(jnp.where(rest == m2, lane, LANES), axis=-1, keepdims=True)
    e2 = jnp.exp(m2 - m1)
    g1 = 1.0 / (1.0 + e2)
    g2 = e2 / (1.0 + e2)

    oh1 = (lane == i1)
    oh2 = (lane == i2)
    sel = jnp.where(oh1 | oh2, 1.0, 0.0).astype(BF16)
    r = lax.broadcasted_iota(I32, (tm, tm), 0)
    c = lax.broadcasted_iota(I32, (tm, tm), 1)
    strict = jnp.where(r > c, 1.0, 0.0).astype(BF16)
    before = _dot(strict, sel) + base_scr[...]
    r1 = jnp.sum(jnp.where(oh1, before, 0.0), axis=-1, keepdims=True)
    r2 = jnp.sum(jnp.where(oh2, before, 0.0), axis=-1, keepdims=True)
    base_scr[...] = base_scr[...] + jnp.sum(sel.astype(F32), axis=0, keepdims=True)

    meta = jnp.zeros((tm, LANES), F32)
    for idx, val in enumerate((i1.astype(F32), i2.astype(F32), g1, g2, r1, r2)):
        meta = jnp.where(lane == idx, val, meta)
    meta_ref[...] = meta
    cnt_ref[...] = base_scr[...]


def _router(x2, g, rwt, rb, n_exp, tm=512):
    T, D = x2.shape
    tm = min(tm, T)
    const = lambda i: (0, 0)
    return pl.pallas_call(
        functools.partial(_router_kernel, n_exp=n_exp),
        grid=(T // tm,),
        in_specs=[
            pl.BlockSpec((tm, D), lambda i: (i, 0)),
            pl.BlockSpec((1, D), const),
            pl.BlockSpec(rwt.shape, const),
            pl.BlockSpec((1, LANES), const),
        ],
        out_specs=[pl.BlockSpec((tm, LANES), lambda i: (i, 0)), pl.BlockSpec((1, LANES), const)],
        out_shape=[jax.ShapeDtypeStruct((T, LANES), F32), jax.ShapeDtypeStruct((1, LANES), F32)],
        scratch_shapes=[pltpu.VMEM((1, LANES), F32)],
        compiler_params=_cparams("arbitrary"),
        name="router",
    )(x2, g, rwt, rb)


def _dispatch_kernel(pos_ref, pad_ref, x_ref, xs_hbm, zbuf, sem, zsem, *, n_exp, tile):
    i = pl.program_id(0)
    tc = x_ref.shape[0]
    T = pos_ref.shape[0] // TOP_K

    def zero_copies(e):
        start, head, rem = pad_ref[3 * e], pad_ref[3 * e + 1], pad_ref[3 * e + 2]
        out = []
        for r in range(7):
            out.append((r < head, pltpu.make_async_copy(
                zbuf.at[pl.ds(0, 1)], xs_hbm.at[pl.ds(start + r, 1)], zsem)))
        for b in range(3, (tile - 1).bit_length()):
            n = 1 << b
            dst = pl.multiple_of(start + head + (rem & (n - 1)), 8)
            out.append((((rem >> b) & 1) == 1, pltpu.make_async_copy(
                zbuf.at[pl.ds(0, n)], xs_hbm.at[pl.ds(dst, n)], zsem)))
        return out

    def tail_copies():
        zr = zbuf.shape[0]
        n_active = pad_ref[3 * n_exp]
        out = []
        for t in range(xs_hbm.shape[0] // tile):
            for r0 in range(0, tile, zr):
                out.append((t >= n_active, pltpu.make_async_copy(
                    zbuf, xs_hbm.at[pl.ds(t * tile + r0, zr)], zsem)))
        return out

    def all_zero_copies():
        return [pc for e in range(n_exp) for pc in zero_copies(e)] + tail_copies()

    @pl.when(i == 0)
    def _():
        zbuf[...] = jnp.zeros_like(zbuf)
        for pred, cp in all_zero_copies():
            @pl.when(pred)
            def _():
                cp.start()

    base = i * tc

    def tok(r, _):
        for k in range(TOP_K):
            slot = pos_ref[k * T + base + r]
            pltpu.make_async_copy(x_ref.at[pl.ds(r, 1)], xs_hbm.at[pl.ds(slot, 1)], sem).start()
        return 0

    lax.fori_loop(0, tc, tok, 0, unroll=8)
    for k in range(TOP_K):
        pltpu.make_async_copy(x_ref, xs_hbm.at[pl.ds(0, tc)], sem).wait()

    @pl.when(i == pl.num_programs(0) - 1)
    def _():
        for pred, cp in all_zero_copies():
            @pl.when(pred)
            def _():
                cp.wait()


def _dispatch(pos, pad, x2, n_rows, n_exp, tile, tc=512):
    T, D = x2.shape
    tc = min(tc, T)
    return pl.pallas_call(
        functools.partial(_dispatch_kernel, n_exp=n_exp, tile=tile),
        grid_spec=pltpu.PrefetchScalarGridSpec(
            num_scalar_prefetch=2,
            grid=(T // tc,),
            in_specs=[pl.BlockSpec((tc, D), lambda i, pos, pad: (i, 0))],
            out_specs=pl.BlockSpec(memory_space=pl.ANY),
            scratch_shapes=[
                pltpu.VMEM((max(tile // 2, 8), D), F32),
                pltpu.SemaphoreType.DMA,
                pltpu.SemaphoreType.DMA,
            ],
        ),
        out_shape=jax.ShapeDtypeStruct((n_rows, D), F32),
        compiler_params=_cparams("arbitrary"),
        name="dispatch",
    )(pos, pad, x2)


def _experts_kernel(te_ref, na_ref, xs_ref, g_ref, wg_ref, wu_ref, wd_ref, y_ref, h_scr, acc_scr):
    t = pl.program_id(0)
    j = pl.program_id(1)
    nj = pl.num_programs(1)

    @pl.when(t < na_ref[0])
    def _():
        @pl.when(j == 0)
        def _():
            h_scr[...] = _rmsnorm(xs_ref[...], g_ref[...]).astype(BF16)
            acc_scr[...] = jnp.zeros_like(acc_scr)

        hb = h_scr[...]
        a = _dot(hb, wg_ref[0, 0].astype(BF16))
        b = _dot(hb, wu_ref[0, 0].astype(BF16))
        act = (_silu(a) * b).astype(BF16)
        acc_scr[...] += _dot(act, wd_ref[0, 0].astype(BF16))

        @pl.when(j == nj - 1)
        def _():
            y_ref[...] = acc_scr[...]

    @pl.when((t >= na_ref[0]) & (j == 0))
    def _():
        y_ref[...] = jnp.zeros_like(y_ref)


def _experts(tile_exp, n_active, xs, g, wg, wu, wd, layer, tile, tf):
    P, D = xs.shape
    F = wg.shape[-1]
    tf = min(tf, F)
    nj = F // tf
    n_tiles = P // tile

    def row_map(t, j, te, na):
        return (jnp.minimum(t, na[0] - 1), 0)

    def jeff(t, j, na):
        return jnp.where(t < na[0], j, nj - 1)

    return pl.pallas_call(
        _experts_kernel,
        grid_spec=pltpu.PrefetchScalarGridSpec(
            num_scalar_prefetch=2,
            grid=(n_tiles, nj),
            in_specs=[
                pl.BlockSpec((tile, D), row_map),
                pl.BlockSpec((1, D), lambda t, j, te, na: (0, 0)),
                pl.BlockSpec((1, 1, D, tf), lambda t, j, te, na: (layer, te[t], 0, jeff(t, j, na))),
                pl.BlockSpec((1, 1, D, tf), lambda t, j, te, na: (layer, te[t], 0, jeff(t, j, na))),
                pl.BlockSpec((1, 1, tf, D), lambda t, j, te, na: (layer, te[t], jeff(t, j, na), 0)),
            ],
            out_specs=pl.BlockSpec((tile, D), lambda t, j, te, na: (t, 0)),
            scratch_shapes=[pltpu.VMEM((tile, D), BF16), pltpu.VMEM((tile, D), F32)],
        ),
        out_shape=jax.ShapeDtypeStruct((P, D), F32),
        compiler_params=_cparams("arbitrary", "arbitrary"),
        name="experts",
    )(tile_exp, n_active, xs, g, wg, wu, wd)


def _combine_kernel(pos_ref, x_ref, meta_ref, y_hbm, o_ref, buf, sem):
    i = pl.program_id(0)
    tc = x_ref.shape[0]
    T = pos_ref.shape[0] // TOP_K
    base = i * tc

    def issue(r, _):
        for k in range(TOP_K):
            slot = pos_ref[k * T + base + r]
            pltpu.make_async_copy(y_hbm.at[pl.ds(slot, 1)], buf.at[k, pl.ds(r, 1)], sem).start()
        return 0

    lax.fori_loop(0, tc, issue, 0, unroll=8)
    for k in range(TOP_K):
        pltpu.make_async_copy(y_hbm.at[pl.ds(0, tc)], buf.at[k], sem).wait()
    meta = meta_ref[...]
    o_ref[...] = x_ref[...] + meta[:, 2:3] * buf[0] + meta[:, 3:4] * buf[1]


def _combine(pos, x2, meta, y, tc=512):
    T, D = x2.shape
    tc = min(tc, T)
    return pl.pallas_call(
        _combine_kernel,
        grid_spec=pltpu.PrefetchScalarGridSpec(
            num_scalar_prefetch=1,
            grid=(T // tc,),
            in_specs=[
                pl.BlockSpec((tc, D), lambda i, pos: (i, 0)),
                pl.BlockSpec((tc, LANES), lambda i, pos: (i, 0)),
                pl.BlockSpec(memory_space=pl.ANY),
            ],
            out_specs=pl.BlockSpec((tc, D), lambda i, pos: (i, 0)),
            scratch_shapes=[pltpu.VMEM((TOP_K, tc, D), F32), pltpu.SemaphoreType.DMA],
        ),
        out_shape=jax.ShapeDtypeStruct((T, D), F32),
        compiler_params=_cparams("arbitrary"),
        name="combine",
    )(pos, x2, meta, y)


def _moe(x2, g, rw, rb, wg, wu, wd, layer):
    T, D = x2.shape
    E = rw.shape[1]
    tile = min(MOE_TILE, T)
    n_tiles = (T * TOP_K) // tile + E
    rwt = jnp.zeros((8 * ((E + 7) // 8), D), F32).at[:E].set(rw.T)
    rbp = jnp.zeros((1, LANES), F32).at[0, :E].set(rb)
    meta, cnt = _router(x2, g, rwt, rbp, E)

    counts = cnt[0, :E].astype(I32)
    tiles_e = (counts + tile - 1) // tile
    tile_end = jnp.cumsum(tiles_e)
    row_off = (tile_end - tiles_e) * tile
    n_active = tile_end[-1:]
    tile_ids = jnp.arange(n_tiles, dtype=I32)
    tile_exp = jnp.minimum(jnp.sum(tile_ids[:, None] >= tile_end[None, :], axis=1), E - 1)
    tile_exp = jnp.where(jnp.arange(n_tiles) < n_active[0], tile_exp, tile_exp[jnp.maximum(n_active[0] - 1, 0)])
    pad_start = row_off + counts
    pad_head = (-pad_start) & 7
    pad_rest = tiles_e * tile - counts - pad_head
    pad = jnp.concatenate([jnp.stack([pad_start, pad_head, pad_rest], axis=1).reshape(-1), n_active]).astype(I32)
    eid = meta[:, 0:TOP_K].astype(I32)
    rank = meta[:, 4:4 + TOP_K].astype(I32)
    pos = (row_off[eid] + rank).T.reshape(-1)

    xs = _dispatch(pos, pad, x2, n_tiles * tile, E, tile)
    y = _experts(tile_exp.astype(I32), n_active.astype(I32), xs, g, wg, wu, wd, layer, tile, MOE_FF_TILE)
    return _combine(pos, x2, meta, y)


def kernel(x, ev_norm_mix, ev_w_in, pool_w, pool_scale, conv_dw_w, conv_dw_b, conv_ln_g, conv_ln_b, ev_w_out, ev_norm_ffn, ffn_w_gate, ffn_w_up, ffn_w_down, od_norm_mix, od_w_in, forget_bias, q_norm_g, k_norm_g, sc_conv_w, od_w_out, od_norm_ffn, router_w, router_b, moe_w_gate, moe_w_up, moe_w_down):
    B, S, D = x.shape
    T = B * S
    depth = ev_norm_mix.shape[0] + od_norm_mix.shape[0]
    A = od_w_in.shape[2] * HEAD_DIM // (6 * HEAD_DIM + 1)
    H = A // HEAD_DIM
    row = lambda v: v.reshape(1, -1)
    head = jnp.arange(A) // HEAD_DIM
    bd = (head[:, None] == head[None, :]).astype(BF16)

    x2 = x.reshape(T, D)
    for layer in range(depth):
        i = layer // 2
        if layer % 2 == 0:
            C = pool_scale.shape[1]
            xa, u = _in_even(x2, row(ev_norm_mix[i]), ev_w_in[i].astype(BF16))
            x3 = _mix_even(
                xa.reshape(B, S, C), u.reshape(B, S, C), x2.reshape(B, S, D),
                pool_w[i].astype(BF16), row(pool_scale[i]), conv_dw_w[i], row(conv_dw_b[i]),
                row(conv_ln_g[i]), row(conv_ln_b[i]), ev_w_out[i].astype(BF16))
            x2 = _ffn(x3.reshape(T, D), row(ev_norm_ffn[i]), ffn_w_gate[i].astype(BF16),
                      ffn_w_up[i].astype(BF16), ffn_w_down[i].astype(BF16))
        else:
            w = od_w_in[i]
            w_main = jnp.concatenate([w[:, :3 * A], w[:, 3 * A + H:]], axis=1).astype(BF16)
            w_f = jnp.zeros((D, LANES), F32).at[:, :H].set(w[:, 3 * A:3 * A + H]).astype(BF16)
            fb = jnp.zeros((1, LANES), F32).at[0, :H].set(forget_bias[i])
            q, k, v, lf, u, sb = _in_odd(
                x2, row(od_norm_mix[i]), w_main, w_f, fb, bd,
                row(jnp.tile(q_norm_g[i], H)), row(jnp.tile(k_norm_g[i], H)))
            sh = lambda t: t.reshape(B, S, -1)
            yc = _attn(sh(q), sh(k), sh(v), sh(lf))
            x3 = _out_odd(yc, sh(u), sh(sb), x2.reshape(B, S, D), sc_conv_w[i], od_w_out[i].astype(BF16))
            x2 = _moe(x3.reshape(T, D), row(od_norm_ffn[i]), router_w[i], router_b[i],
                      moe_w_gate, moe_w_up, moe_w_down, i)
    return x2.reshape(B, S, D)
```

```python
import functools

import jax
import jax.numpy as jnp
from jax import lax
from jax.experimental import pallas as pl
from jax.experimental.pallas import tpu as pltpu

F32 = jnp.float32
BF16 = jnp.bfloat16
I32 = jnp.int32

NORM_EPS = 1e-6
LANES = 128
SUBLANES = 8
POOL_WINDOWS = (2, 4, 8, 16)
CONV_WIDTH = 31
SC_WIDTH = 3
HEAD_DIM = 64
TOP_K = 2
LOG2E = 1.4426950408889634
MIX_HALO = 32
SC_HALO = 16
MOE_TILE = 1024
MOE_FF_TILE = 512
VMEM_LIMIT = 56 * 1024 * 1024


def _cparams(*sem):
    return pltpu.CompilerParams(dimension_semantics=sem, vmem_limit_bytes=VMEM_LIMIT)


def _rmsnorm(x, g):
    return x * lax.rsqrt(jnp.mean(x * x, axis=-1, keepdims=True) + NORM_EPS) * g


def _sigmoid(x):
    return 1.0 / (1.0 + jnp.exp(-x))


def _silu(x):
    return x * _sigmoid(x)


def _dot(a, b):
    return jnp.dot(a, b, preferred_element_type=F32)


def _in_even_kernel(x_ref, g_ref, w_ref, xa_ref, u_ref):
    c = xa_ref.shape[-1]
    hb = _rmsnorm(x_ref[...], g_ref[...]).astype(BF16)
    xa_ref[...] = _dot(hb, w_ref[:, 0:c]).astype(BF16)
    val = _dot(hb, w_ref[:, c:2 * c])
    gate = _dot(hb, w_ref[:, 2 * c:3 * c])
    u_ref[...] = (val * _sigmoid(gate)).astype(BF16)


def _in_even(x2, g, w, tm=512):
    T, D = x2.shape
    C = w.shape[1] // 3
    tm = min(tm, T)
    return pl.pallas_call(
        _in_even_kernel,
        grid=(T // tm,),
        in_specs=[
            pl.BlockSpec((tm, D), lambda i: (i, 0)),
            pl.BlockSpec((1, D), lambda i: (0, 0)),
            pl.BlockSpec((D, 3 * C), lambda i: (0, 0)),
        ],
        out_specs=[
            pl.BlockSpec((tm, C), lambda i: (i, 0)),
            pl.BlockSpec((tm, C), lambda i: (i, 0)),
        ],
        out_shape=[jax.ShapeDtypeStruct((T, C), BF16)] * 2,
        compiler_params=_cparams("parallel"),
        name="in_even",
    )(x2, g, w)


def _mix_even_kernel(xa_ref, xah_ref, u_ref, uh_ref, x_ref, pw_ref, ps_ref, dww_ref,
                     dwb_ref, lng_ref, lnb_ref, wo_ref, o_ref, xa_ext, u_ext, mix, shf):
    i = pl.program_id(1)
    ts = xa_ref.shape[1]
    C = xa_ref.shape[2]
    H = MIX_HALO

    @pl.when(i == 0)
    def _():
        xa_ext[0:H, :] = jnp.zeros((H, C), F32)
        u_ext[0:H, :] = jnp.zeros((H, C), F32)

    @pl.when(i > 0)
    def _():
        xa_ext[0:H, :] = xah_ref[0].astype(F32)
        u_ext[0:H, :] = uh_ref[0].astype(F32)

    xa_ext[H:H + ts, :] = xa_ref[0].astype(F32)
    u_ext[H:H + ts, :] = u_ref[0].astype(F32)

    t_glob = i * ts + lax.broadcasted_iota(I32, (ts, 1), 0)
    for g, win in enumerate(POOL_WINDOWS):
        lo = g * LANES
        acc = xa_ext[H:H + ts, lo:lo + LANES]
        cur = acc
        for j in range(1, win):
            acc = acc + xa_ext[H - j:H - j + ts, lo:lo + LANES]
        cnt = jnp.minimum(t_glob + 1, win).astype(F32)
        pooled = acc / cnt - cur
        y = _dot(pooled.astype(BF16), pw_ref[g]) * ps_ref[:, lo:lo + LANES]
        mix[:, lo:lo + LANES] = y.astype(BF16)

    CH = 64
    SUB = 8
    n_a = (CONV_WIDTH - 1) // SUB + 1
    back = SUB * (n_a - 1)
    for r0 in range(0, ts, CH):
        acc = jnp.zeros((CH, C), F32) + dwb_ref[...]
        for b in range(SUB):
            lo = H - back - b + r0
            shf[...] = u_ext[lo:lo + CH + back, :]
            for a in range(n_a):
                shift = SUB * a + b
                if shift < CONV_WIDTH:
                    k = CONV_WIDTH - 1 - shift
                    acc = acc + dww_ref[k:k + 1, :] * shf[back - SUB * a:back - SUB * a + CH, :]
        mu = jnp.mean(acc, axis=-1, keepdims=True)
        xc = acc - mu
        var = jnp.mean(xc * xc, axis=-1, keepdims=True)
        yn = xc * lax.rsqrt(var + NORM_EPS) * lng_ref[...] + lnb_ref[...]
        mix[r0:r0 + CH, C:2 * C] = _silu(yn).astype(BF16)

    o_ref[0] = x_ref[0] + _dot(mix[...], wo_ref[...])


def _mix_even(xa, u, x, pw, ps, dww, dwb, lng, lnb, wo, ts=512):
    B, S, C = xa.shape
    D = x.shape[-1]
    ts = min(ts, S)
    hb = ts // MIX_HALO
    main = lambda b, i: (b, i, 0)
    halo = lambda b, i: (b, jnp.maximum(i * hb - 1, 0), 0)
    const2 = lambda b, i: (0, 0)
    return pl.pallas_call(
        _mix_even_kernel,
        grid=(B, S // ts),
        in_specs=[
            pl.BlockSpec((1, ts, C), main),
            pl.BlockSpec((1, MIX_HALO, C), halo),
            pl.BlockSpec((1, ts, C), main),
            pl.BlockSpec((1, MIX_HALO, C), halo),
            pl.BlockSpec((1, ts, D), main),
            pl.BlockSpec(pw.shape, lambda b, i: (0, 0, 0)),
            pl.BlockSpec((1, C), const2),
            pl.BlockSpec(dww.shape, const2),
            pl.BlockSpec((1, C), const2),
            pl.BlockSpec((1, C), const2),
            pl.BlockSpec((1, C), const2),
            pl.BlockSpec(wo.shape, const2),
        ],
        out_specs=pl.BlockSpec((1, ts, D), main),
        out_shape=jax.ShapeDtypeStruct((B, S, D), F32),
        scratch_shapes=[
            pltpu.VMEM((MIX_HALO + ts, C), F32),
            pltpu.VMEM((MIX_HALO + ts, C), F32),
            pltpu.VMEM((ts, 2 * C), BF16),
            pltpu.VMEM((64 + 8 * ((CONV_WIDTH - 1) // 8), C), F32),
        ],
        compiler_params=_cparams("parallel", "arbitrary"),
        name="mix_even",
    )(xa, xa, u, u, x, pw, ps, dww, dwb, lng, lnb, wo)


def _ffn_kernel(x_ref, g_ref, wg_ref, wu_ref, wd_ref, o_ref, h_scr, act_scr, *, fc):
    x = x_ref[...]
    h_scr[...] = _rmsnorm(x, g_ref[...]).astype(BF16)
    F = wg_ref.shape[1]
    for c0 in range(0, F, fc):
        hb = h_scr[...]
        a = _dot(hb, wg_ref[:, c0:c0 + fc])
        b = _dot(hb, wu_ref[:, c0:c0 + fc])
        act_scr[:, c0:c0 + fc] = (_silu(a) * b).astype(BF16)
    o_ref[...] = x + _dot(act_scr[...], wd_ref[...])


def _ffn(x2, g, wg, wu, wd, tm=512, fc=256):
    T, D = x2.shape
    F = wg.shape[1]
    tm = min(tm, T)
    fc = min(fc, F)
    const = lambda i: (0, 0)
    return pl.pallas_call(
        functools.partial(_ffn_kernel, fc=fc),
        grid=(T // tm,),
        in_specs=[
            pl.BlockSpec((tm, D), lambda i: (i, 0)),
            pl.BlockSpec((1, D), const),
            pl.BlockSpec((D, F), const),
            pl.BlockSpec((D, F), const),
            pl.BlockSpec((F, D), const),
        ],
        out_specs=pl.BlockSpec((tm, D), lambda i: (i, 0)),
        out_shape=jax.ShapeDtypeStruct((T, D), F32),
        scratch_shapes=[pltpu.VMEM((tm, D), BF16), pltpu.VMEM((tm, F), BF16)],
        compiler_params=_cparams("parallel"),
        name="ffn",
    )(x2, g, wg, wu, wd)


def _in_odd_kernel(x_ref, g_ref, w_ref, wf_ref, fb_ref, bd_ref, gq_ref, gk_ref,
                   q_ref, k_ref, v_ref, lf_ref, u_ref, sb_ref):
    A = q_ref.shape[-1]
    hb = _rmsnorm(x_ref[...], g_ref[...]).astype(BF16)

    def headnorm(t, gain):
        ss = _dot((t * t).astype(BF16), bd_ref[...])
        return t * lax.rsqrt(ss * (1.0 / HEAD_DIM) + NORM_EPS) * gain

    q = _dot(hb, w_ref[:, 0:A])
    q_ref[...] = (headnorm(q, gq_ref[...]) * (HEAD_DIM ** -0.5 * LOG2E)).astype(BF16)
    k = _dot(hb, w_ref[:, A:2 * A])
    k_ref[...] = headnorm(k, gk_ref[...]).astype(BF16)
    v_ref[...] = _dot(hb, w_ref[:, 2 * A:3 * A]).astype(BF16)
    z = _dot(hb, wf_ref[...]) + fb_ref[...]
    lf_ref[...] = (jnp.minimum(z, 0.0) - jnp.log1p(jnp.exp(-jnp.abs(z)))) * LOG2E
    sx = _dot(hb, w_ref[:, 3 * A:4 * A])
    sb_ref[...] = _dot(hb, w_ref[:, 4 * A:5 * A]).astype(BF16)
    sc = _dot(hb, w_ref[:, 5 * A:6 * A])
    u_ref[...] = (sc * sx).astype(BF16)


def _in_odd(x2, g, w, wf, fb, bd, gq, gk, tm=512):
    T, D = x2.shape
    A = w.shape[1] // 6
    tm = min(tm, T)
    const = lambda i: (0, 0)
    row = lambda i: (i, 0)
    return pl.pallas_call(
        _in_odd_kernel,
        grid=(T // tm,),
        in_specs=[
            pl.BlockSpec((tm, D), row),
            pl.BlockSpec((1, D), const),
            pl.BlockSpec(w.shape, const),
            pl.BlockSpec(wf.shape, const),
            pl.BlockSpec((1, LANES), const),
            pl.BlockSpec(bd.shape, const),
            pl.BlockSpec((1, A), const),
            pl.BlockSpec((1, A), const),
        ],
        out_specs=[
            pl.BlockSpec((tm, A), row), pl.BlockSpec((tm, A), row), pl.BlockSpec((tm, A), row),
            pl.BlockSpec((tm, LANES), row),
            pl.BlockSpec((tm, A), row), pl.BlockSpec((tm, A), row),
        ],
        out_shape=[
            jax.ShapeDtypeStruct((T, A), BF16), jax.ShapeDtypeStruct((T, A), BF16),
            jax.ShapeDtypeStruct((T, A), BF16), jax.ShapeDtypeStruct((T, LANES), F32),
            jax.ShapeDtypeStruct((T, A), BF16), jax.ShapeDtypeStruct((T, A), BF16),
        ],
        compiler_params=_cparams("parallel"),
        name="in_odd",
    )(x2, g, w, wf, fb, bd, gq, gk)


def _attn_kernel(q_ref, k_ref, v_ref, lf_ref, o_ref, qa_scr, ka_scr, *, tq, tk):
    p = pl.program_id(1)
    qi = pl.program_id(2)
    S = k_ref.shape[1]
    lane = lax.broadcasted_iota(I32, (1, LANES), 1)

    def head_lanes(x, hl):
        return x if hl == 0 else pltpu.roll(x, HEAD_DIM, axis=1)

    @pl.when(qi == 0)
    def _():
        f = lf_ref[0]
        row = lax.broadcasted_iota(I32, (S, 1), 0)
        sh = 1
        while sh < S:
            f = f + jnp.where(row >= sh, pltpu.roll(f, sh, axis=0), 0.0)
            sh *= 2
        kf = k_ref[0].astype(F32)
        for hl in range(2):
            hg = 2 * p + hl
            col = jnp.sum(jnp.where(lane == hg, f, 0.0), axis=-1, keepdims=True)
            parts = []
            rem = col
            for _ in range(3):
                part = rem.astype(BF16).astype(F32)
                parts.append(part)
                rem = rem - part
            qa = jnp.zeros((S, LANES), F32)
            ka = head_lanes(kf, hl)
            for n, part in enumerate(parts):
                qa = jnp.where(lane == HEAD_DIM + n, part, qa)
                qa = jnp.where(lane == HEAD_DIM + 3 + n, 1.0, qa)
                ka = jnp.where(lane == HEAD_DIM + n, 1.0, ka)
                ka = jnp.where(lane == HEAD_DIM + 3 + n, -part, ka)
            ka = jnp.where(lane >= HEAD_DIM + 6, 0.0, ka)
            qa_scr[hl] = qa.astype(BF16)
            ka_scr[hl] = ka.astype(BF16)

    q0 = pl.multiple_of(qi * tq, tq)
    qf = q_ref[0].astype(F32)
    rowi = lax.broadcasted_iota(I32, (tq, tk), 0)
    coli = lax.broadcasted_iota(I32, (tq, tk), 1)
    qh = [jnp.where(lane < HEAD_DIM, head_lanes(qf, hl),
                    qa_scr[hl, pl.ds(q0, tq), :].astype(F32)).astype(BF16) for hl in range(2)]

    def scores(hl, k0):
        kb = ka_scr[hl, pl.ds(k0, tk), :]
        return lax.dot_general(qh[hl], kb, (((1,), (1,)), ((), ())), preferred_element_type=F32)

    carry = []
    for hl in range(2):
        s = jnp.where(rowi >= coli, scores(hl, q0), -jnp.inf)
        m = jnp.max(s, axis=-1, keepdims=True)
        pr = jnp.exp2(s - m)
        l = jnp.sum(pr, axis=-1, keepdims=True)
        carry += [m, l, _dot(pr.astype(BF16), v_ref[0, pl.ds(q0, tk), :])]

    def step(k0, carry, mask):
        vb = v_ref[0, pl.ds(k0, tk), :]
        out = []
        for hl in range(2):
            m, l, acc = carry[3 * hl:3 * hl + 3]
            s = scores(hl, k0)
            if mask is not None:
                s = jnp.where(mask, s, -jnp.inf)
            m_new = jnp.maximum(m, jnp.max(s, axis=-1, keepdims=True))
            alpha = jnp.exp2(m - m_new)
            pr = jnp.exp2(s - m_new)
            l = alpha * l + jnp.sum(pr, axis=-1, keepdims=True)
            out += [m_new, l, alpha * acc + _dot(pr.astype(BF16), vb)]
        return tuple(out)

    carry = tuple(carry)
    for d in range(1, tq // tk):
        carry = step(q0 + d * tk, carry, rowi >= coli + d * tk)
    carry = lax.fori_loop(0, qi * (tq // tk),
                          lambda j, c: step(pl.multiple_of(j * tk, tk), c, None), carry)
    outs = [carry[3 * hl + 2] / carry[3 * hl + 1] for hl in range(2)]
    o_ref[0] = jnp.where(lane < HEAD_DIM, outs[0], outs[1]).astype(BF16)


def _attn(q, k, v, lf, tq=512, tk=512):
    B, S, A = q.shape
    tq = min(tq, S)
    tk = min(tk, tq)
    npair = A // LANES
    return pl.pallas_call(
        functools.partial(_attn_kernel, tq=tq, tk=tk),
        grid=(B, npair, S // tq),
        in_specs=[
            pl.BlockSpec((1, tq, LANES), lambda b, p, i: (b, i, p)),
            pl.BlockSpec((1, S, LANES), lambda b, p, i: (b, 0, p)),
            pl.BlockSpec((1, S, LANES), lambda b, p, i: (b, 0, p)),
            pl.BlockSpec((1, S, LANES), lambda b, p, i: (b, 0, 0)),
        ],
        out_specs=pl.BlockSpec((1, tq, LANES), lambda b, p, i: (b, i, p)),
        out_shape=jax.ShapeDtypeStruct((B, S, A), BF16),
        scratch_shapes=[pltpu.VMEM((2, S, LANES), BF16), pltpu.VMEM((2, S, LANES), BF16)],
        compiler_params=_cparams("parallel", "parallel", "arbitrary"),
        name="attn",
    )(q, k, v, lf)


def _out_odd_kernel(yc_ref, u_ref, uh_ref, sb_ref, x_ref, cw_ref, wo_ref, o_ref, u_ext, mix):
    i = pl.program_id(1)
    ts = u_ref.shape[1]
    C = u_ref.shape[2]
    H = SC_HALO

    @pl.when(i == 0)
    def _():
        u_ext[0:H, :] = jnp.zeros((H, C), F32)

    @pl.when(i > 0)
    def _():
        u_ext[0:H, :] = uh_ref[0].astype(F32)

    u_ext[H:H + ts, :] = u_ref[0].astype(F32)
    conv = jnp.zeros((ts, C), F32)
    for k in range(SC_WIDTH):
        off = H - (SC_WIDTH - 1) + k
        conv = conv + cw_ref[k:k + 1, :] * u_ext[off:off + ts, :]
    mix[:, 0:C] = yc_ref[0]
    mix[:, C:2 * C] = (sb_ref[0].astype(F32) * conv).astype(BF16)
    o_ref[0] = x_ref[0] + _dot(mix[...], wo_ref[...])


def _out_odd(yc, u, sb, x, cw, wo, ts=512):
    B, S, C = u.shape
    D = x.shape[-1]
    ts = min(ts, S)
    hb = ts // SC_HALO
    main = lambda b, i: (b, i, 0)
    halo = lambda b, i: (b, jnp.maximum(i * hb - 1, 0), 0)
    const2 = lambda b, i: (0, 0)
    return pl.pallas_call(
        _out_odd_kernel,
        grid=(B, S // ts),
        in_specs=[
            pl.BlockSpec((1, ts, C), main),
            pl.BlockSpec((1, ts, C), main),
            pl.BlockSpec((1, SC_HALO, C), halo),
            pl.BlockSpec((1, ts, C), main),
            pl.BlockSpec((1, ts, D), main),
            pl.BlockSpec(cw.shape, const2),
            pl.BlockSpec(wo.shape, const2),
        ],
        out_specs=pl.BlockSpec((1, ts, D), main),
        out_shape=jax.ShapeDtypeStruct((B, S, D), F32),
        scratch_shapes=[pltpu.VMEM((SC_HALO + ts, C), F32), pltpu.VMEM((ts, 2 * C), BF16)],
        compiler_params=_cparams("parallel", "arbitrary"),
        name="out_odd",
    )(yc, u, u, sb, x, cw, wo)


def _router_kernel(x_ref, g_ref, rw_ref, rb_ref, meta_ref, cnt_ref, base_scr, *, n_exp):
    i = pl.program_id(0)
    tm = x_ref.shape[0]

    @pl.when(i == 0)
    def _():
        base_scr[...] = jnp.zeros_like(base_scr)

    h = _rmsnorm(x_ref[...], g_ref[...])
    lane = lax.broadcasted_iota(I32, (1, LANES), 1)
    logits = jnp.full((tm, LANES), -jnp.inf, F32)
    for e in range(n_exp):
        s = jnp.sum(h * rw_ref[e:e + 1, :], axis=-1, keepdims=True)
        logits = jnp.where(lane == e, s, logits)
    logits = logits + rb_ref[...]

    m1 = jnp.max(logits, axis=-1, keepdims=True)
    i1 = jnp.min(jnp.where(logits == m1, lane, LANES), axis=-1, keepdims=True)
    rest = jnp.where(lane == i1, -jnp.inf, logits)
    m2 = jnp.max(rest, axis=-1, keepdims=True)
    i2 = jnp.min(jnp.where(rest == m2, lane, LANES), axis=-1, keepdims=True)
    e2 = jnp.exp(m2 - m1)
    g1 = 1.0 / (1.0 + e2)
    g2 = e2 / (1.0 + e2)

    oh1 = (lane == i1)
    oh2 = (lane == i2)
    sel = jnp.where(oh1 | oh2, 1.0, 0.0).astype(BF16)
    r = lax.broadcasted_iota(I32, (tm, tm), 0)
    c = lax.broadcasted_iota(I32, (tm, tm), 1)
    strict = jnp.where(r > c, 1.0, 0.0).astype(BF16)
    before = _dot(strict, sel) + base_scr[...]
    r1 = jnp.sum(jnp.where(oh1, before, 0.0), axis=-1, keepdims=True)
    r2 = jnp.sum(jnp.where(oh2, before, 0.0), axis=-1, keepdims=True)
    base_scr[...] = base_scr[...] + jnp.sum(sel.astype(F32), axis=0, keepdims=True)

    meta = jnp.zeros((tm, LANES), F32)
    for idx, val in enumerate((i1.astype(F32), i2.astype(F32), g1, g2, r1, r2)):
        meta = jnp.where(lane == idx, val, meta)
    meta_ref[...] = meta
    cnt_ref[...] = base_scr[...]


def _router(x2, g, rwt, rb, n_exp, tm=512):
    T, D = x2.shape
    tm = min(tm, T)
    const = lambda i: (0, 0)
    return pl.pallas_call(
        functools.partial(_router_kernel, n_exp=n_exp),
        grid=(T // tm,),
        in_specs=[
            pl.BlockSpec((tm, D), lambda i: (i, 0)),
            pl.BlockSpec((1, D), const),
            pl.BlockSpec(rwt.shape, const),
            pl.BlockSpec((1, LANES), const),
        ],
        out_specs=[pl.BlockSpec((tm, LANES), lambda i: (i, 0)), pl.BlockSpec((1, LANES), const)],
        out_shape=[jax.ShapeDtypeStruct((T, LANES), F32), jax.ShapeDtypeStruct((1, LANES), F32)],
        scratch_shapes=[pltpu.VMEM((1, LANES), F32)],
        compiler_params=_cparams("arbitrary"),
        name="router",
    )(x2, g, rwt, rb)


def _dispatch_kernel(pos_ref, pad_ref, x_ref, xs_hbm, zbuf, sem, zsem, *, n_exp, tile):
    i = pl.program_id(0)
    tc = x_ref.shape[0]
    T = pos_ref.shape[0] // TOP_K

    def zero_copies(e):
        start, head, rem = pad_ref[3 * e], pad_ref[3 * e + 1], pad_ref[3 * e + 2]
        out = []
        for r in range(7):
            out.append((r < head, pltpu.make_async_copy(
                zbuf.at[pl.ds(0, 1)], xs_hbm.at[pl.ds(start + r, 1)], zsem)))
        for b in range(3, (tile - 1).bit_length()):
            n = 1 << b
            dst = pl.multiple_of(start + head + (rem & (n - 1)), 8)
            out.append((((rem >> b) & 1) == 1, pltpu.make_async_copy(
                zbuf.at[pl.ds(0, n)], xs_hbm.at[pl.ds(dst, n)], zsem)))
        return out

    def tail_copies():
        zr = zbuf.shape[0]
        n_active = pad_ref[3 * n_exp]
        out = []
        for t in range(xs_hbm.shape[0] // tile):
            for r0 in range(0, tile, zr):
                out.append((t >= n_active, pltpu.make_async_copy(
                    zbuf, xs_hbm.at[pl.ds(t * tile + r0, zr)], zsem)))
        return out

    def all_zero_copies():
        return [pc for e in range(n_exp) for pc in zero_copies(e)] + tail_copies()

    @pl.when(i == 0)
    def _():
        zbuf[...] = jnp.zeros_like(zbuf)
        for pred, cp in all_zero_copies():
            @pl.when(pred)
            def _():
                cp.start()

    base = i * tc

    def group(g, _):
        r0 = pl.multiple_of(g * SUBLANES, SUBLANES)
        for s in range(SUBLANES):
            for k in range(TOP_K):
                slot = pos_ref[k * T + base + r0 + s]
                pltpu.make_async_copy(x_ref.at[pl.ds(r0 + s, 1)], xs_hbm.at[pl.ds(slot, 1)], sem).start()
        return 0

    lax.fori_loop(0, tc // SUBLANES, group, 0)
    for k in range(TOP_K):
        pltpu.make_async_copy(x_ref, xs_hbm.at[pl.ds(0, tc)], sem).wait()

    @pl.when(i == pl.num_programs(0) - 1)
    def _():
        for pred, cp in all_zero_copies():
            @pl.when(pred)
            def _():
                cp.wait()


def _dispatch(pos, pad, x2, n_rows, n_exp, tile, tc=512):
    T, D = x2.shape
    tc = min(tc, T)
    return pl.pallas_call(
        functools.partial(_dispatch_kernel, n_exp=n_exp, tile=tile),
        grid_spec=pltpu.PrefetchScalarGridSpec(
            num_scalar_prefetch=2,
            grid=(T // tc,),
            in_specs=[pl.BlockSpec((tc, D), lambda i, pos, pad: (i, 0))],
            out_specs=pl.BlockSpec(memory_space=pl.ANY),
            scratch_shapes=[
                pltpu.VMEM((max(tile // 2, 8), D), F32),
                pltpu.SemaphoreType.DMA,
                pltpu.SemaphoreType.DMA,
            ],
        ),
        out_shape=jax.ShapeDtypeStruct((n_rows, D), F32),
        compiler_params=_cparams("arbitrary"),
        name="dispatch",
    )(pos, pad, x2)


def _experts_kernel(te_ref, na_ref, tr_ref, xs_ref, g_ref, wg_ref, wu_ref, wd_ref, y_ref, h_scr, acc_scr):
    t = pl.program_id(0)
    j = pl.program_id(1)
    nj = pl.num_programs(1)
    tile = xs_ref.shape[0]
    half = tile // 2

    def ff_step(rows):
        hb = h_scr[0:rows, :]
        a = _dot(hb, wg_ref[0, 0].astype(BF16))
        b = _dot(hb, wu_ref[0, 0].astype(BF16))
        act = (_silu(a) * b).astype(BF16)
        acc_scr[0:rows, :] += _dot(act, wd_ref[0, 0].astype(BF16))

    @pl.when(t < na_ref[0])
    def _():
        @pl.when(j == 0)
        def _():
            h_scr[...] = _rmsnorm(xs_ref[...], g_ref[...]).astype(BF16)
            acc_scr[...] = jnp.zeros_like(acc_scr)

        @pl.when(tr_ref[t] > half)
        def _():
            ff_step(tile)

        @pl.when(tr_ref[t] <= half)
        def _():
            ff_step(half)

        @pl.when(j == nj - 1)
        def _():
            y_ref[...] = acc_scr[...]

    @pl.when((t >= na_ref[0]) & (j == 0))
    def _():
        y_ref[...] = jnp.zeros_like(y_ref)


def _experts(tile_exp, n_active, tile_rows, xs, g, wg, wu, wd, layer, tile, tf):
    P, D = xs.shape
    F = wg.shape[-1]
    tf = min(tf, F)
    nj = F // tf
    n_tiles = P // tile

    def row_map(t, j, te, na, tr):
        return (jnp.minimum(t, na[0] - 1), 0)

    def jeff(t, j, na):
        return jnp.where(t < na[0], j, nj - 1)

    return pl.pallas_call(
        _experts_kernel,
        grid_spec=pltpu.PrefetchScalarGridSpec(
            num_scalar_prefetch=3,
            grid=(n_tiles, nj),
            in_specs=[
                pl.BlockSpec((tile, D), row_map),
                pl.BlockSpec((1, D), lambda t, j, te, na, tr: (0, 0)),
                pl.BlockSpec((1, 1, D, tf), lambda t, j, te, na, tr: (layer, te[t], 0, jeff(t, j, na))),
                pl.BlockSpec((1, 1, D, tf), lambda t, j, te, na, tr: (layer, te[t], 0, jeff(t, j, na))),
                pl.BlockSpec((1, 1, tf, D), lambda t, j, te, na, tr: (layer, te[t], jeff(t, j, na), 0)),
            ],
            out_specs=pl.BlockSpec((tile, D), lambda t, j, te, na, tr: (t, 0)),
            scratch_shapes=[pltpu.VMEM((tile, D), BF16), pltpu.VMEM((tile, D), F32)],
        ),
        out_shape=jax.ShapeDtypeStruct((P, D), F32),
        compiler_params=_cparams("arbitrary", "arbitrary"),
        name="experts",
    )(tile_exp, n_active, tile_rows, xs, g, wg, wu, wd)


def _combine_kernel(pos_ref, x_ref, meta_ref, y_hbm, o_ref, buf, sem):
    i = pl.program_id(0)
    tc = x_ref.shape[0]
    T = pos_ref.shape[0] // TOP_K
    base = i * tc

    def issue(r, _):
        for k in range(TOP_K):
            slot = pos_ref[k * T + base + r]
            pltpu.make_async_copy(y_hbm.at[pl.ds(slot, 1)], buf.at[k, pl.ds(r, 1)], sem).start()
        return 0

    lax.fori_loop(0, tc, issue, 0, unroll=8)
    for k in range(TOP_K):
        pltpu.make_async_copy(y_hbm.at[pl.ds(0, tc)], buf.at[k], sem).wait()
    meta = meta_ref[...]
    o_ref[...] = x_ref[...] + meta[:, 2:3] * buf[0] + meta[:, 3:4] * buf[1]


def _combine(pos, x2, meta, y, tc=512):
    T, D = x2.shape
    tc = min(tc, T)
    return pl.pallas_call(
        _combine_kernel,
        grid_spec=pltpu.PrefetchScalarGridSpec(
            num_scalar_prefetch=1,
            grid=(T // tc,),
            in_specs=[
                pl.BlockSpec((tc, D), lambda i, pos: (i, 0)),
                pl.BlockSpec((tc, LANES), lambda i, pos: (i, 0)),
                pl.BlockSpec(memory_space=pl.ANY),
            ],
            out_specs=pl.BlockSpec((tc, D), lambda i, pos: (i, 0)),
            scratch_shapes=[pltpu.VMEM((TOP_K, tc, D), F32), pltpu.SemaphoreType.DMA],
        ),
        out_shape=jax.ShapeDtypeStruct((T, D), F32),
        compiler_params=_cparams("arbitrary"),
        name="combine",
    )(pos, x2, meta, y)


def _moe(x2, g, rw, rb, wg, wu, wd, layer):
    T, D = x2.shape
    E = rw.shape[1]
    tile = min(MOE_TILE, T)
    n_tiles = (T * TOP_K) // tile + E
    rwt = jnp.zeros((8 * ((E + 7) // 8), D), F32).at[:E].set(rw.T)
    rbp = jnp.zeros((1, LANES), F32).at[0, :E].set(rb)
    meta, cnt = _router(x2, g, rwt, rbp, E)

    counts = cnt[0, :E].astype(I32)
    tiles_e = (counts + tile - 1) // tile
    tile_end = jnp.cumsum(tiles_e)
    row_off = (tile_end - tiles_e) * tile
    n_active = tile_end[-1:]
    tile_ids = jnp.arange(n_tiles, dtype=I32)
    tile_exp = jnp.minimum(jnp.sum(tile_ids[:, None] >= tile_end[None, :], axis=1), E - 1)
    tile_exp = jnp.where(jnp.arange(n_tiles) < n_active[0], tile_exp, tile_exp[jnp.maximum(n_active[0] - 1, 0)])
    pad_start = row_off + counts
    pad_head = (-pad_start) & 7
    pad_rest = tiles_e * tile - counts - pad_head
    pad = jnp.concatenate([jnp.stack([pad_start, pad_head, pad_rest], axis=1).reshape(-1), n_active]).astype(I32)
    eid = meta[:, 0:TOP_K].astype(I32)
    rank = meta[:, 4:4 + TOP_K].astype(I32)
    pos = (row_off[eid] + rank).T.reshape(-1)

    xs = _dispatch(pos, pad, x2, n_tiles * tile, E, tile)
    tile_start = (tile_end - tiles_e)[tile_exp]
    tile_rows = jnp.clip(counts[tile_exp] - (tile_ids - tile_start) * tile, 0, tile)
    y = _experts(tile_exp.astype(I32), n_active.astype(I32), tile_rows.astype(I32),
                 xs, g, wg, wu, wd, layer, tile, MOE_FF_TILE)
    return _combine(pos, x2, meta, y)


def kernel(x, ev_norm_mix, ev_w_in, pool_w, pool_scale, conv_dw_w, conv_dw_b, conv_ln_g, conv_ln_b, ev_w_out, ev_norm_ffn, ffn_w_gate, ffn_w_up, ffn_w_down, od_norm_mix, od_w_in, forget_bias, q_norm_g, k_norm_g, sc_conv_w, od_w_out, od_norm_ffn, router_w, router_b, moe_w_gate, moe_w_up, moe_w_down):
    B, S, D = x.shape
    T = B * S
    depth = ev_norm_mix.shape[0] + od_norm_mix.shape[0]
    A = od_w_in.shape[2] * HEAD_DIM // (6 * HEAD_DIM + 1)
    H = A // HEAD_DIM
    row = lambda v: v.reshape(1, -1)
    head = jnp.arange(A) // HEAD_DIM
    bd = (head[:, None] == head[None, :]).astype(BF16)

    x2 = x.reshape(T, D)
    for layer in range(depth):
        i = layer // 2
        if layer % 2 == 0:
            C = pool_scale.shape[1]
            xa, u = _in_even(x2, row(ev_norm_mix[i]), ev_w_in[i].astype(BF16))
            x3 = _mix_even(
                xa.reshape(B, S, C), u.reshape(B, S, C), x2.reshape(B, S, D),
                pool_w[i].astype(BF16), row(pool_scale[i]), conv_dw_w[i], row(conv_dw_b[i]),
                row(conv_ln_g[i]), row(conv_ln_b[i]), ev_w_out[i].astype(BF16))
            x2 = _ffn(x3.reshape(T, D), row(ev_norm_ffn[i]), ffn_w_gate[i].astype(BF16),
                      ffn_w_up[i].astype(BF16), ffn_w_down[i].astype(BF16))
        else:
            w = od_w_in[i]
            w_main = jnp.concatenate([w[:, :3 * A], w[:, 3 * A + H:]], axis=1).astype(BF16)
            w_f = jnp.zeros((D, LANES), F32).at[:, :H].set(w[:, 3 * A:3 * A + H]).astype(BF16)
            fb = jnp.zeros((1, LANES), F32).at[0, :H].set(forget_bias[i])
            q, k, v, lf, u, sb = _in_odd(
                x2, row(od_norm_mix[i]), w_main, w_f, fb, bd,
                row(jnp.tile(q_norm_g[i], H)), row(jnp.tile(k_norm_g[i], H)))
            sh = lambda t: t.reshape(B, S, -1)
            yc = _attn(sh(q), sh(k), sh(v), sh(lf))
            x3 = _out_odd(yc, sh(u), sh(sb), x2.reshape(B, S, D), sc_conv_w[i], od_w_out[i].astype(BF16))
            x2 = _moe(x3.reshape(T, D), row(od_norm_ffn[i]), router_w[i], router_b[i],
                      moe_w_gate, moe_w_up, moe_w_down, i)
    return x2.reshape(B, S, D)
```

```python
import functools

import jax
import jax.numpy as jnp
from jax import lax
from jax.experimental import pallas as pl
from jax.experimental.pallas import tpu as pltpu

F32 = jnp.float32
BF16 = jnp.bfloat16
I32 = jnp.int32

NORM_EPS = 1e-6
LANES = 128
SUBLANES = 8
POOL_WINDOWS = (2, 4, 8, 16)
CONV_WIDTH = 31
SC_WIDTH = 3
HEAD_DIM = 64
TOP_K = 2
LOG2E = 1.4426950408889634
MIX_HALO = 32
SC_HALO = 16
MOE_TILE = 1024
MOE_FF_TILE = 512
VMEM_LIMIT = 56 * 1024 * 1024


def _cparams(*sem):
    return pltpu.CompilerParams(dimension_semantics=sem, vmem_limit_bytes=VMEM_LIMIT)


def _rmsnorm(x, g):
    return x * lax.rsqrt(jnp.mean(x * x, axis=-1, keepdims=True) + NORM_EPS) * g


def _sigmoid(x):
    return 1.0 / (1.0 + jnp.exp(-x))


def _silu(x):
    return x * _sigmoid(x)


def _dot(a, b):
    return jnp.dot(a, b, preferred_element_type=F32)


def _in_even_kernel(x_ref, g_ref, w_ref, xa_ref, u_ref):
    c = xa_ref.shape[-1]
    hb = _rmsnorm(x_ref[...], g_ref[...]).astype(BF16)
    xa_ref[...] = _dot(hb, w_ref[:, 0:c]).astype(BF16)
    val = _dot(hb, w_ref[:, c:2 * c])
    gate = _dot(hb, w_ref[:, 2 * c:3 * c])
    u_ref[...] = (val * _sigmoid(gate)).astype(BF16)


def _in_even(x2, g, w, tm=512):
    T, D = x2.shape
    C = w.shape[1] // 3
    tm = min(tm, T)
    return pl.pallas_call(
        _in_even_kernel,
        grid=(T // tm,),
        in_specs=[
            pl.BlockSpec((tm, D), lambda i: (i, 0)),
            pl.BlockSpec((1, D), lambda i: (0, 0)),
            pl.BlockSpec((D, 3 * C), lambda i: (0, 0)),
        ],
        out_specs=[
            pl.BlockSpec((tm, C), lambda i: (i, 0)),
            pl.BlockSpec((tm, C), lambda i: (i, 0)),
        ],
        out_shape=[jax.ShapeDtypeStruct((T, C), BF16)] * 2,
        compiler_params=_cparams("parallel"),
        name="in_even",
    )(x2, g, w)


def _mix_even_kernel(xa_ref, xah_ref, u_ref, uh_ref, x_ref, pw_ref, ps_ref, dww_ref,
                     dwb_ref, lng_ref, lnb_ref, wo_ref, o_ref, xa_ext, u_ext, mix, shf):
    i = pl.program_id(1)
    ts = xa_ref.shape[1]
    C = xa_ref.shape[2]
    H = MIX_HALO

    @pl.when(i == 0)
    def _():
        xa_ext[0:H, :] = jnp.zeros((H, C), F32)
        u_ext[0:H, :] = jnp.zeros((H, C), F32)

    @pl.when(i > 0)
    def _():
        xa_ext[0:H, :] = xah_ref[0].astype(F32)
        u_ext[0:H, :] = uh_ref[0].astype(F32)

    xa_ext[H:H + ts, :] = xa_ref[0].astype(F32)
    u_ext[H:H + ts, :] = u_ref[0].astype(F32)

    t_glob = i * ts + lax.broadcasted_iota(I32, (ts, 1), 0)
    for g, win in enumerate(POOL_WINDOWS):
        lo = g * LANES
        acc = xa_ext[H:H + ts, lo:lo + LANES]
        cur = acc
        for j in range(1, win):
            acc = acc + xa_ext[H - j:H - j + ts, lo:lo + LANES]
        cnt = jnp.minimum(t_glob + 1, win).astype(F32)
        pooled = acc / cnt - cur
        y = _dot(pooled.astype(BF16), pw_ref[g]) * ps_ref[:, lo:lo + LANES]
        mix[:, lo:lo + LANES] = y.astype(BF16)

    CH = 64
    SUB = 8
    n_a = (CONV_WIDTH - 1) // SUB + 1
    back = SUB * (n_a - 1)
    for r0 in range(0, ts, CH):
        acc = jnp.zeros((CH, C), F32) + dwb_ref[...]
        for b in range(SUB):
            lo = H - back - b + r0
            shf[...] = u_ext[lo:lo + CH + back, :]
            for a in range(n_a):
                shift = SUB * a + b
                if shift < CONV_WIDTH:
                    k = CONV_WIDTH - 1 - shift
                    acc = acc + dww_ref[k:k + 1, :] * shf[back - SUB * a:back - SUB * a + CH, :]
        mu = jnp.mean(acc, axis=-1, keepdims=True)
        xc = acc - mu
        var = jnp.mean(xc * xc, axis=-1, keepdims=True)
        yn = xc * lax.rsqrt(var + NORM_EPS) * lng_ref[...] + lnb_ref[...]
        mix[r0:r0 + CH, C:2 * C] = _silu(yn).astype(BF16)

    o_ref[0] = x_ref[0] + _dot(mix[...], wo_ref[...])


def _mix_even(xa, u, x, pw, ps, dww, dwb, lng, lnb, wo, ts=512):
    B, S, C = xa.shape
    D = x.shape[-1]
    ts = min(ts, S)
    hb = ts // MIX_HALO
    main = lambda b, i: (b, i, 0)
    halo = lambda b, i: (b, jnp.maximum(i * hb - 1, 0), 0)
    const2 = lambda b, i: (0, 0)
    return pl.pallas_call(
        _mix_even_kernel,
        grid=(B, S // ts),
        in_specs=[
            pl.BlockSpec((1, ts, C), main),
            pl.BlockSpec((1, MIX_HALO, C), halo),
            pl.BlockSpec((1, ts, C), main),
            pl.BlockSpec((1, MIX_HALO, C), halo),
            pl.BlockSpec((1, ts, D), main),
            pl.BlockSpec(pw.shape, lambda b, i: (0, 0, 0)),
            pl.BlockSpec((1, C), const2),
            pl.BlockSpec(dww.shape, const2),
            pl.BlockSpec((1, C), const2),
            pl.BlockSpec((1, C), const2),
            pl.BlockSpec((1, C), const2),
            pl.BlockSpec(wo.shape, const2),
        ],
        out_specs=pl.BlockSpec((1, ts, D), main),
        out_shape=jax.ShapeDtypeStruct((B, S, D), F32),
        scratch_shapes=[
            pltpu.VMEM((MIX_HALO + ts, C), F32),
            pltpu.VMEM((MIX_HALO + ts, C), F32),
            pltpu.VMEM((ts, 2 * C), BF16),
            pltpu.VMEM((64 + 8 * ((CONV_WIDTH - 1) // 8), C), F32),
        ],
        compiler_params=_cparams("parallel", "arbitrary"),
        name="mix_even",
    )(xa, xa, u, u, x, pw, ps, dww, dwb, lng, lnb, wo)


def _ffn_kernel(x_ref, g_ref, wg_ref, wu_ref, wd_ref, o_ref, h_scr, act_scr, *, fc):
    x = x_ref[...]
    h_scr[...] = _rmsnorm(x, g_ref[...]).astype(BF16)
    F = wg_ref.shape[1]
    for c0 in range(0, F, fc):
        hb = h_scr[...]
        a = _dot(hb, wg_ref[:, c0:c0 + fc])
        b = _dot(hb, wu_ref[:, c0:c0 + fc])
        act_scr[:, c0:c0 + fc] = (_silu(a) * b).astype(BF16)
    o_ref[...] = x + _dot(act_scr[...], wd_ref[...])


def _ffn(x2, g, wg, wu, wd, tm=512, fc=256):
    T, D = x2.shape
    F = wg.shape[1]
    tm = min(tm, T)
    fc = min(fc, F)
    const = lambda i: (0, 0)
    return pl.pallas_call(
        functools.partial(_ffn_kernel, fc=fc),
        grid=(T // tm,),
        in_specs=[
            pl.BlockSpec((tm, D), lambda i: (i, 0)),
            pl.BlockSpec((1, D), const),
            pl.BlockSpec((D, F), const),
            pl.BlockSpec((D, F), const),
            pl.BlockSpec((F, D), const),
        ],
        out_specs=pl.BlockSpec((tm, D), lambda i: (i, 0)),
        out_shape=jax.ShapeDtypeStruct((T, D), F32),
        scratch_shapes=[pltpu.VMEM((tm, D), BF16), pltpu.VMEM((tm, F), BF16)],
        compiler_params=_cparams("parallel"),
        name="ffn",
    )(x2, g, wg, wu, wd)


def _in_odd_kernel(x_ref, g_ref, w_ref, wf_ref, fb_ref, bd_ref, gq_ref, gk_ref,
                   q_ref, k_ref, v_ref, lf_ref, u_ref, sb_ref):
    A = q_ref.shape[-1]
    hb = _rmsnorm(x_ref[...], g_ref[...]).astype(BF16)

    def headnorm(t, gain):
        ss = _dot((t * t).astype(BF16), bd_ref[...])
        return t * lax.rsqrt(ss * (1.0 / HEAD_DIM) + NORM_EPS) * gain

    q = _dot(hb, w_ref[:, 0:A])
    q_ref[...] = (headnorm(q, gq_ref[...]) * (HEAD_DIM ** -0.5 * LOG2E)).astype(BF16)
    k = _dot(hb, w_ref[:, A:2 * A])
    k_ref[...] = headnorm(k, gk_ref[...]).astype(BF16)
    v_ref[...] = _dot(hb, w_ref[:, 2 * A:3 * A]).astype(BF16)
    z = _dot(hb, wf_ref[...]) + fb_ref[...]
    lf_ref[...] = (jnp.minimum(z, 0.0) - jnp.log1p(jnp.exp(-jnp.abs(z)))) * LOG2E
    sx = _dot(hb, w_ref[:, 3 * A:4 * A])
    sb_ref[...] = _dot(hb, w_ref[:, 4 * A:5 * A]).astype(BF16)
    sc = _dot(hb, w_ref[:, 5 * A:6 * A])
    u_ref[...] = (sc * sx).astype(BF16)


def _in_odd(x2, g, w, wf, fb, bd, gq, gk, tm=512):
    T, D = x2.shape
    A = w.shape[1] // 6
    tm = min(tm, T)
    const = lambda i: (0, 0)
    row = lambda i: (i, 0)
    return pl.pallas_call(
        _in_odd_kernel,
        grid=(T // tm,),
        in_specs=[
            pl.BlockSpec((tm, D), row),
            pl.BlockSpec((1, D), const),
            pl.BlockSpec(w.shape, const),
            pl.BlockSpec(wf.shape, const),
            pl.BlockSpec((1, LANES), const),
            pl.BlockSpec(bd.shape, const),
            pl.BlockSpec((1, A), const),
            pl.BlockSpec((1, A), const),
        ],
        out_specs=[
            pl.BlockSpec((tm, A), row), pl.BlockSpec((tm, A), row), pl.BlockSpec((tm, A), row),
            pl.BlockSpec((tm, LANES), row),
            pl.BlockSpec((tm, A), row), pl.BlockSpec((tm, A), row),
        ],
        out_shape=[
            jax.ShapeDtypeStruct((T, A), BF16), jax.ShapeDtypeStruct((T, A), BF16),
            jax.ShapeDtypeStruct((T, A), BF16), jax.ShapeDtypeStruct((T, LANES), F32),
            jax.ShapeDtypeStruct((T, A), BF16), jax.ShapeDtypeStruct((T, A), BF16),
        ],
        compiler_params=_cparams("parallel"),
        name="in_odd",
    )(x2, g, w, wf, fb, bd, gq, gk)


def _attn_kernel(q_ref, k_ref, v_ref, lf_ref, o_ref, qa_scr, ka_scr, *, tq, tk):
    p = pl.program_id(1)
    qi = pl.program_id(2)
    S = k_ref.shape[1]
    lane = lax.broadcasted_iota(I32, (1, LANES), 1)

    def head_lanes(x, hl):
        return x if hl == 0 else pltpu.roll(x, HEAD_DIM, axis=1)

    @pl.when(qi == 0)
    def _():
        f = lf_ref[0]
        row = lax.broadcasted_iota(I32, (S, 1), 0)
        sh = 1
        while sh < S:
            f = f + jnp.where(row >= sh, pltpu.roll(f, sh, axis=0), 0.0)
            sh *= 2
        kf = k_ref[0].astype(F32)
        for hl in range(2):
            hg = 2 * p + hl
            col = jnp.sum(jnp.where(lane == hg, f, 0.0), axis=-1, keepdims=True)
            parts = []
            rem = col
            for _ in range(3):
                part = rem.astype(BF16).astype(F32)
                parts.append(part)
                rem = rem - part
            qa = jnp.zeros((S, LANES), F32)
            ka = head_lanes(kf, hl)
            for n, part in enumerate(parts):
                qa = jnp.where(lane == HEAD_DIM + n, part, qa)
                qa = jnp.where(lane == HEAD_DIM + 3 + n, 1.0, qa)
                ka = jnp.where(lane == HEAD_DIM + n, 1.0, ka)
                ka = jnp.where(lane == HEAD_DIM + 3 + n, -part, ka)
            ka = jnp.where(lane >= HEAD_DIM + 6, 0.0, ka)
            qa_scr[hl] = qa.astype(BF16)
            ka_scr[hl] = ka.astype(BF16)

    q0 = pl.multiple_of(qi * tq, tq)
    qf = q_ref[0].astype(F32)
    rowi = lax.broadcasted_iota(I32, (tq, tk), 0)
    coli = lax.broadcasted_iota(I32, (tq, tk), 1)
    qh = [jnp.where(lane < HEAD_DIM, head_lanes(qf, hl),
                    qa_scr[hl, pl.ds(q0, tq), :].astype(F32)).astype(BF16) for hl in range(2)]

    def scores(hl, k0):
        kb = ka_scr[hl, pl.ds(k0, tk), :]
        return lax.dot_general(qh[hl], kb, (((1,), (1,)), ((), ())), preferred_element_type=F32)

    carry = []
    for hl in range(2):
        s = jnp.where(rowi >= coli, scores(hl, q0), -jnp.inf)
        m = jnp.max(s, axis=-1, keepdims=True)
        pr = jnp.exp2(s - m)
        l = jnp.sum(pr, axis=-1, keepdims=True)
        carry += [m, l, _dot(pr.astype(BF16), v_ref[0, pl.ds(q0, tk), :])]

    def step(k0, carry, mask):
        vb = v_ref[0, pl.ds(k0, tk), :]
        out = []
        for hl in range(2):
            m, l, acc = carry[3 * hl:3 * hl + 3]
            s = scores(hl, k0)
            if mask is not None:
                s = jnp.where(mask, s, -jnp.inf)
            m_new = jnp.maximum(m, jnp.max(s, axis=-1, keepdims=True))
            alpha = jnp.exp2(m - m_new)
            pr = jnp.exp2(s - m_new)
            l = alpha * l + jnp.sum(pr, axis=-1, keepdims=True)
            out += [m_new, l, alpha * acc + _dot(pr.astype(BF16), vb)]
        return tuple(out)

    carry = tuple(carry)
    for d in range(1, tq // tk):
        carry = step(q0 + d * tk, carry, rowi >= coli + d * tk)
    carry = lax.fori_loop(0, qi * (tq // tk),
                          lambda j, c: step(pl.multiple_of(j * tk, tk), c, None), carry)
    outs = [carry[3 * hl + 2] / carry[3 * hl + 1] for hl in range(2)]
    o_ref[0] = jnp.where(lane < HEAD_DIM, outs[0], outs[1]).astype(BF16)


def _attn(q, k, v, lf, tq=512, tk=512):
    B, S, A = q.shape
    tq = min(tq, S)
    tk = min(tk, tq)
    npair = A // LANES
    return pl.pallas_call(
        functools.partial(_attn_kernel, tq=tq, tk=tk),
        grid=(B, npair, S // tq),
        in_specs=[
            pl.BlockSpec((1, tq, LANES), lambda b, p, i: (b, i, p)),
            pl.BlockSpec((1, S, LANES), lambda b, p, i: (b, 0, p)),
            pl.BlockSpec((1, S, LANES), lambda b, p, i: (b, 0, p)),
            pl.BlockSpec((1, S, LANES), lambda b, p, i: (b, 0, 0)),
        ],
        out_specs=pl.BlockSpec((1, tq, LANES), lambda b, p, i: (b, i, p)),
        out_shape=jax.ShapeDtypeStruct((B, S, A), BF16),
        scratch_shapes=[pltpu.VMEM((2, S, LANES), BF16), pltpu.VMEM((2, S, LANES), BF16)],
        compiler_params=_cparams("parallel", "parallel", "arbitrary"),
        name="attn",
    )(q, k, v, lf)


def _out_odd_kernel(yc_ref, u_ref, uh_ref, sb_ref, x_ref, cw_ref, wo_ref, o_ref, u_ext, mix):
    i = pl.program_id(1)
    ts = u_ref.shape[1]
    C = u_ref.shape[2]
    H = SC_HALO

    @pl.when(i == 0)
    def _():
        u_ext[0:H, :] = jnp.zeros((H, C), F32)

    @pl.when(i > 0)
    def _():
        u_ext[0:H, :] = uh_ref[0].astype(F32)

    u_ext[H:H + ts, :] = u_ref[0].astype(F32)
    conv = jnp.zeros((ts, C), F32)
    for k in range(SC_WIDTH):
        off = H - (SC_WIDTH - 1) + k
        conv = conv + cw_ref[k:k + 1, :] * u_ext[off:off + ts, :]
    mix[:, 0:C] = yc_ref[0]
    mix[:, C:2 * C] = (sb_ref[0].astype(F32) * conv).astype(BF16)
    o_ref[0] = x_ref[0] + _dot(mix[...], wo_ref[...])


def _out_odd(yc, u, sb, x, cw, wo, ts=512):
    B, S, C = u.shape
    D = x.shape[-1]
    ts = min(ts, S)
    hb = ts // SC_HALO
    main = lambda b, i: (b, i, 0)
    halo = lambda b, i: (b, jnp.maximum(i * hb - 1, 0), 0)
    const2 = lambda b, i: (0, 0)
    return pl.pallas_call(
        _out_odd_kernel,
        grid=(B, S // ts),
        in_specs=[
            pl.BlockSpec((1, ts, C), main),
            pl.BlockSpec((1, ts, C), main),
            pl.BlockSpec((1, SC_HALO, C), halo),
            pl.BlockSpec((1, ts, C), main),
            pl.BlockSpec((1, ts, D), main),
            pl.BlockSpec(cw.shape, const2),
            pl.BlockSpec(wo.shape, const2),
        ],
        out_specs=pl.BlockSpec((1, ts, D), main),
        out_shape=jax.ShapeDtypeStruct((B, S, D), F32),
        scratch_shapes=[pltpu.VMEM((SC_HALO + ts, C), F32), pltpu.VMEM((ts, 2 * C), BF16)],
        compiler_params=_cparams("parallel", "arbitrary"),
        name="out_odd",
    )(yc, u, u, sb, x, cw, wo)


def _router_kernel(x_ref, g_ref, rw_ref, rb_ref, meta_ref, cnt_ref, base_scr, *, n_exp):
    i = pl.program_id(0)
    tm = x_ref.shape[0]

    @pl.when(i == 0)
    def _():
        base_scr[...] = jnp.zeros_like(base_scr)

    h = _rmsnorm(x_ref[...], g_ref[...])
    lane = lax.broadcasted_iota(I32, (1, LANES), 1)
    h_hi = h.astype(BF16)
    h_lo = (h - h_hi.astype(F32)).astype(BF16)
    logits = _dot(h_hi, rw_ref[0]) + _dot(h_hi, rw_ref[1]) + _dot(h_lo, rw_ref[0])
    logits = jnp.where(lane < n_exp, logits + rb_ref[...], -jnp.inf)

    m1 = jnp.max(logits, axis=-1, keepdims=True)
    i1 = jnp.min(jnp.where(logits == m1, lane, LANES), axis=-1, keepdims=True)
    rest = jnp.where(lane == i1, -jnp.inf, logits)
    m2 = jnp.max(rest, axis=-1, keepdims=True)
    i2 = jnp.min(jnp.where(rest == m2, lane, LANES), axis=-1, keepdims=True)
    e2 = jnp.exp(m2 - m1)
    g1 = 1.0 / (1.0 + e2)
    g2 = e2 / (1.0 + e2)

    oh1 = (lane == i1)
    oh2 = (lane == i2)
    sel = jnp.where(oh1 | oh2, 1.0, 0.0).astype(BF16)
    r = lax.broadcasted_iota(I32, (tm, tm), 0)
    c = lax.broadcasted_iota(I32, (tm, tm), 1)
    strict = jnp.where(r > c, 1.0, 0.0).astype(BF16)
    before = _dot(strict, sel) + base_scr[...]
    r1 = jnp.sum(jnp.where(oh1, before, 0.0), axis=-1, keepdims=True)
    r2 = jnp.sum(jnp.where(oh2, before, 0.0), axis=-1, keepdims=True)
    base_scr[...] = base_scr[...] + jnp.sum(sel.astype(F32), axis=0, keepdims=True)

    meta = jnp.zeros((tm, LANES), F32)
    for idx, val in enumerate((i1.astype(F32), i2.astype(F32), g1, g2, r1, r2)):
        meta = jnp.where(lane == idx, val, meta)
    meta_ref[...] = meta
    cnt_ref[...] = base_scr[...]


def _router(x2, g, rwt, rb, n_exp, tm=512):
    T, D = x2.shape
    tm = min(tm, T)
    const = lambda i: (0, 0)
    return pl.pallas_call(
        functools.partial(_router_kernel, n_exp=n_exp),
        grid=(T // tm,),
        in_specs=[
            pl.BlockSpec((tm, D), lambda i: (i, 0)),
            pl.BlockSpec((1, D), const),
            pl.BlockSpec(rwt.shape, lambda i: (0, 0, 0)),
            pl.BlockSpec((1, LANES), const),
        ],
        out_specs=[pl.BlockSpec((tm, LANES), lambda i: (i, 0)), pl.BlockSpec((1, LANES), const)],
        out_shape=[jax.ShapeDtypeStruct((T, LANES), F32), jax.ShapeDtypeStruct((1, LANES), F32)],
        scratch_shapes=[pltpu.VMEM((1, LANES), F32)],
        compiler_params=_cparams("arbitrary"),
        name="router",
    )(x2, g, rwt, rb)


def _dispatch_kernel(pos_ref, pad_ref, x_ref, xs_hbm, zbuf, sem, zsem, *, n_exp, tile):
    i = pl.program_id(0)
    tc = x_ref.shape[0] * SUBLANES
    T = pos_ref.shape[0] // TOP_K

    def zero_copies(e):
        start, head, rem = pad_ref[3 * e], pad_ref[3 * e + 1], pad_ref[3 * e + 2]
        out = []
        for r in range(7):
            out.append((r < head, pltpu.make_async_copy(
                zbuf.at[pl.ds(0, 1)], xs_hbm.at[pl.ds(start + r, 1)], zsem)))
        for b in range(3, (tile - 1).bit_length()):
            n = 1 << b
            dst = pl.multiple_of(start + head + (rem & (n - 1)), 8)
            out.append((((rem >> b) & 1) == 1, pltpu.make_async_copy(
                zbuf.at[pl.ds(0, n)], xs_hbm.at[pl.ds(dst, n)], zsem)))
        return out

    def tail_copies():
        zr = zbuf.shape[0]
        n_active = pad_ref[3 * n_exp]
        out = []
        for t in range(xs_hbm.shape[0] // tile):
            for r0 in range(0, tile, zr):
                out.append((t >= n_active, pltpu.make_async_copy(
                    zbuf, xs_hbm.at[pl.ds(t * tile + r0, zr)], zsem)))
        return out

    def all_zero_copies():
        return [pc for e in range(n_exp) for pc in zero_copies(e)] + tail_copies()

    @pl.when(i == 0)
    def _():
        zbuf[...] = jnp.zeros_like(zbuf)
        for pred, cp in all_zero_copies():
            @pl.when(pred)
            def _():
                cp.start()

    base = i * tc

    def group(g, _):
        for s in range(SUBLANES):
            for k in range(TOP_K):
                slot = pos_ref[k * T + base + g * SUBLANES + s]
                pltpu.make_async_copy(x_ref.at[g, pl.ds(s, 1)], xs_hbm.at[pl.ds(slot, 1)], sem).start()
        return 0

    lax.fori_loop(0, tc // SUBLANES, group, 0)
    for k in range(TOP_K):
        pltpu.make_async_copy(x_ref, x_ref, sem).wait()

    @pl.when(i == pl.num_programs(0) - 1)
    def _():
        for pred, cp in all_zero_copies():
            @pl.when(pred)
            def _():
                cp.wait()


def _dispatch(pos, pad, x2, n_rows, n_exp, tile, tc=512):
    T, D = x2.shape
    tc = min(tc, T)
    return pl.pallas_call(
        functools.partial(_dispatch_kernel, n_exp=n_exp, tile=tile),
        grid_spec=pltpu.PrefetchScalarGridSpec(
            num_scalar_prefetch=2,
            grid=(T // tc,),
            in_specs=[pl.BlockSpec((tc // SUBLANES, SUBLANES, D), lambda i, pos, pad: (i, 0, 0))],
            out_specs=pl.BlockSpec(memory_space=pl.ANY),
            scratch_shapes=[
                pltpu.VMEM((max(tile // 2, 8), D), F32),
                pltpu.SemaphoreType.DMA,
                pltpu.SemaphoreType.DMA,
            ],
        ),
        out_shape=jax.ShapeDtypeStruct((n_rows, D), F32),
        compiler_params=_cparams("arbitrary"),
        name="dispatch",
    )(pos, pad, x2.reshape(T // SUBLANES, SUBLANES, D))


def _experts_kernel(te_ref, na_ref, tr_ref, xs_ref, g_ref, wg_ref, wu_ref, wd_ref, y_ref, h_scr, acc_scr):
    t = pl.program_id(0)
    j = pl.program_id(1)
    nj = pl.num_programs(1)
    tile = xs_ref.shape[0]
    half = tile // 2

    def ff_step(rows):
        hb = h_scr[0:rows, :]
        a = _dot(hb, wg_ref[0, 0].astype(BF16))
        b = _dot(hb, wu_ref[0, 0].astype(BF16))
        act = (_silu(a) * b).astype(BF16)
        acc_scr[0:rows, :] += _dot(act, wd_ref[0, 0].astype(BF16))

    @pl.when(t < na_ref[0])
    def _():
        @pl.when(j == 0)
        def _():
            h_scr[...] = _rmsnorm(xs_ref[...], g_ref[...]).astype(BF16)
            acc_scr[...] = jnp.zeros_like(acc_scr)

        @pl.when(tr_ref[t] > half)
        def _():
            ff_step(tile)

        @pl.when(tr_ref[t] <= half)
        def _():
            ff_step(half)

        @pl.when(j == nj - 1)
        def _():
            y_ref[...] = acc_scr[...]

    @pl.when((t >= na_ref[0]) & (j == 0))
    def _():
        y_ref[...] = jnp.zeros_like(y_ref)


def _experts(tile_exp, n_active, tile_rows, xs, g, wg, wu, wd, layer, tile, tf):
    P, D = xs.shape
    F = wg.shape[-1]
    tf = min(tf, F)
    nj = F // tf
    n_tiles = P // tile

    def row_map(t, j, te, na, tr):
        return (jnp.minimum(t, na[0] - 1), 0)

    def jeff(t, j, na):
        return jnp.where(t < na[0], j, nj - 1)

    return pl.pallas_call(
        _experts_kernel,
        grid_spec=pltpu.PrefetchScalarGridSpec(
            num_scalar_prefetch=3,
            grid=(n_tiles, nj),
            in_specs=[
                pl.BlockSpec((tile, D), row_map),
                pl.BlockSpec((1, D), lambda t, j, te, na, tr: (0, 0)),
                pl.BlockSpec((1, 1, D, tf), lambda t, j, te, na, tr: (layer, te[t], 0, jeff(t, j, na))),
                pl.BlockSpec((1, 1, D, tf), lambda t, j, te, na, tr: (layer, te[t], 0, jeff(t, j, na))),
                pl.BlockSpec((1, 1, tf, D), lambda t, j, te, na, tr: (layer, te[t], jeff(t, j, na), 0)),
            ],
            out_specs=pl.BlockSpec((tile, D), lambda t, j, te, na, tr: (t, 0)),
            scratch_shapes=[pltpu.VMEM((tile, D), BF16), pltpu.VMEM((tile, D), F32)],
        ),
        out_shape=jax.ShapeDtypeStruct((P, D), F32),
        compiler_params=_cparams("arbitrary", "arbitrary"),
        name="experts",
    )(tile_exp, n_active, tile_rows, xs, g, wg, wu, wd)


def _combine_kernel(pos_ref, x_ref, meta_ref, y_hbm, o_ref, buf, sem):
    i = pl.program_id(0)
    tc = x_ref.shape[0]
    T = pos_ref.shape[0] // TOP_K
    base = i * tc

    def group(g, _):
        for s in range(SUBLANES):
            for k in range(TOP_K):
                slot = pos_ref[k * T + base + g * SUBLANES + s]
                pltpu.make_async_copy(y_hbm.at[pl.ds(slot, 1)], buf.at[k, g, pl.ds(s, 1)], sem).start()
        return 0

    lax.fori_loop(0, tc // SUBLANES, group, 0)
    for k in range(TOP_K):
        pltpu.make_async_copy(buf.at[k], buf.at[k], sem).wait()
    meta = meta_ref[...]
    y0 = buf[0].reshape(tc, -1)
    y1 = buf[1].reshape(tc, -1)
    o_ref[...] = x_ref[...] + meta[:, 2:3] * y0 + meta[:, 3:4] * y1


def _combine(pos, x2, meta, y, tc=512):
    T, D = x2.shape
    tc = min(tc, T)
    return pl.pallas_call(
        _combine_kernel,
        grid_spec=pltpu.PrefetchScalarGridSpec(
            num_scalar_prefetch=1,
            grid=(T // tc,),
            in_specs=[
                pl.BlockSpec((tc, D), lambda i, pos: (i, 0)),
                pl.BlockSpec((tc, LANES), lambda i, pos: (i, 0)),
                pl.BlockSpec(memory_space=pl.ANY),
            ],
            out_specs=pl.BlockSpec((tc, D), lambda i, pos: (i, 0)),
            scratch_shapes=[pltpu.VMEM((TOP_K, tc // SUBLANES, SUBLANES, D), F32), pltpu.SemaphoreType.DMA],
        ),
        out_shape=jax.ShapeDtypeStruct((T, D), F32),
        compiler_params=_cparams("arbitrary"),
        name="combine",
    )(pos, x2, meta, y)


def _moe(x2, g, rw, rb, wg, wu, wd, layer):
    T, D = x2.shape
    E = rw.shape[1]
    tile = min(MOE_TILE, T)
    n_tiles = (T * TOP_K) // tile + E
    rwp = jnp.zeros((D, LANES), F32).at[:, :E].set(rw)
    rw_hi = rwp.astype(BF16)
    rwt = jnp.stack([rw_hi, (rwp - rw_hi.astype(F32)).astype(BF16)])
    rbp = jnp.zeros((1, LANES), F32).at[0, :E].set(rb)
    meta, cnt = _router(x2, g, rwt, rbp, E)

    counts = cnt[0, :E].astype(I32)
    tiles_e = (counts + tile - 1) // tile
    tile_end = jnp.cumsum(tiles_e)
    row_off = (tile_end - tiles_e) * tile
    n_active = tile_end[-1:]
    tile_ids = jnp.arange(n_tiles, dtype=I32)
    tile_exp = jnp.minimum(jnp.sum(tile_ids[:, None] >= tile_end[None, :], axis=1), E - 1)
    tile_exp = jnp.where(jnp.arange(n_tiles) < n_active[0], tile_exp, tile_exp[jnp.maximum(n_active[0] - 1, 0)])
    pad_start = row_off + counts
    pad_head = (-pad_start) & 7
    pad_rest = tiles_e * tile - counts - pad_head
    pad = jnp.concatenate([jnp.stack([pad_start, pad_head, pad_rest], axis=1).reshape(-1), n_active]).astype(I32)
    eid = meta[:, 0:TOP_K].astype(I32)
    rank = meta[:, 4:4 + TOP_K].astype(I32)
    pos = (row_off[eid] + rank).T.reshape(-1)

    xs = _dispatch(pos, pad, x2, n_tiles * tile, E, tile)
    tile_start = (tile_end - tiles_e)[tile_exp]
    tile_rows = jnp.clip(counts[tile_exp] - (tile_ids - tile_start) * tile, 0, tile)
    y = _experts(tile_exp.astype(I32), n_active.astype(I32), tile_rows.astype(I32),
                 xs, g, wg, wu, wd, layer, tile, MOE_FF_TILE)
    return _combine(pos, x2, meta, y)


def kernel(x, ev_norm_mix, ev_w_in, pool_w, pool_scale, conv_dw_w, conv_dw_b, conv_ln_g, conv_ln_b, ev_w_out, ev_norm_ffn, ffn_w_gate, ffn_w_up, ffn_w_down, od_norm_mix, od_w_in, forget_bias, q_norm_g, k_norm_g, sc_conv_w, od_w_out, od_norm_ffn, router_w, router_b, moe_w_gate, moe_w_up, moe_w_down):
    B, S, D = x.shape
    T = B * S
    depth = ev_norm_mix.shape[0] + od_norm_mix.shape[0]
    A = od_w_in.shape[2] * HEAD_DIM // (6 * HEAD_DIM + 1)
    H = A // HEAD_DIM
    row = lambda v: v.reshape(1, -1)
    head = jnp.arange(A) // HEAD_DIM
    bd = (head[:, None] == head[None, :]).astype(BF16)

    x2 = x.reshape(T, D)
    for layer in range(depth):
        i = layer // 2
        if layer % 2 == 0:
            C = pool_scale.shape[1]
            xa, u = _in_even(x2, row(ev_norm_mix[i]), ev_w_in[i].astype(BF16))
            x3 = _mix_even(
                xa.reshape(B, S, C), u.reshape(B, S, C), x2.reshape(B, S, D),
                pool_w[i].astype(BF16), row(pool_scale[i]), conv_dw_w[i], row(conv_dw_b[i]),
                row(conv_ln_g[i]), row(conv_ln_b[i]), ev_w_out[i].astype(BF16))
            x2 = _ffn(x3.reshape(T, D), row(ev_norm_ffn[i]), ffn_w_gate[i].astype(BF16),
                      ffn_w_up[i].astype(BF16), ffn_w_down[i].astype(BF16))
        else:
            w = od_w_in[i]
            w_main = jnp.concatenate([w[:, :3 * A], w[:, 3 * A + H:]], axis=1).astype(BF16)
            w_f = jnp.zeros((D, LANES), F32).at[:, :H].set(w[:, 3 * A:3 * A + H]).astype(BF16)
            fb = jnp.zeros((1, LANES), F32).at[0, :H].set(forget_bias[i])
            q, k, v, lf, u, sb = _in_odd(
                x2, row(od_norm_mix[i]), w_main, w_f, fb, bd,
                row(jnp.tile(q_norm_g[i], H)), row(jnp.tile(k_norm_g[i], H)))
            sh = lambda t: t.reshape(B, S, -1)
            yc = _attn(sh(q), sh(k), sh(v), sh(lf))
            x3 = _out_odd(yc, sh(u), sh(sb), x2.reshape(B, S, D), sc_conv_w[i], od_w_out[i].astype(BF16))
            x2 = _moe(x3.reshape(T, D), row(od_norm_ffn[i]), router_w[i], router_b[i],
                      moe_w_gate, moe_w_up, moe_w_down, i)
    return x2.reshape(B, S, D)
```

```python
import functools

import jax
import jax.numpy as jnp
from jax import lax
from jax.experimental import pallas as pl
from jax.experimental.pallas import tpu as pltpu

F32 = jnp.float32
BF16 = jnp.bfloat16
I32 = jnp.int32

NORM_EPS = 1e-6
LANES = 128
SUBLANES = 8
POOL_WINDOWS = (2, 4, 8, 16)
CONV_WIDTH = 31
SC_WIDTH = 3
HEAD_DIM = 64
TOP_K = 2
LOG2E = 1.4426950408889634
MIX_HALO = 32
SC_HALO = 16
MOE_TILE = 1024
MOE_FF_TILE = 512
VMEM_LIMIT = 56 * 1024 * 1024


def _cparams(*sem):
    return pltpu.CompilerParams(dimension_semantics=sem, vmem_limit_bytes=VMEM_LIMIT)


def _rmsnorm(x, g):
    return x * lax.rsqrt(jnp.mean(x * x, axis=-1, keepdims=True) + NORM_EPS) * g


def _sigmoid(x):
    return 1.0 / (1.0 + jnp.exp(-x))


def _silu(x):
    return x * _sigmoid(x)


def _dot(a, b):
    return jnp.dot(a, b, preferred_element_type=F32)


def _in_even_kernel(x_ref, g_ref, w_ref, xa_ref, u_ref):
    c = xa_ref.shape[-1]
    hb = _rmsnorm(x_ref[...], g_ref[...]).astype(BF16)
    xa_ref[...] = _dot(hb, w_ref[:, 0:c]).astype(BF16)
    val = _dot(hb, w_ref[:, c:2 * c])
    gate = _dot(hb, w_ref[:, 2 * c:3 * c])
    u_ref[...] = (val * _sigmoid(gate)).astype(BF16)


def _in_even(x2, g, w, tm=512):
    T, D = x2.shape
    C = w.shape[1] // 3
    tm = min(tm, T)
    return pl.pallas_call(
        _in_even_kernel,
        grid=(T // tm,),
        in_specs=[
            pl.BlockSpec((tm, D), lambda i: (i, 0)),
            pl.BlockSpec((1, D), lambda i: (0, 0)),
            pl.BlockSpec((D, 3 * C), lambda i: (0, 0)),
        ],
        out_specs=[
            pl.BlockSpec((tm, C), lambda i: (i, 0)),
            pl.BlockSpec((tm, C), lambda i: (i, 0)),
        ],
        out_shape=[jax.ShapeDtypeStruct((T, C), BF16)] * 2,
        compiler_params=_cparams("parallel"),
        name="in_even",
    )(x2, g, w)


def _mix_even_kernel(xa_ref, xah_ref, u_ref, uh_ref, x_ref, pw_ref, ps_ref, dww_ref,
                     dwb_ref, lng_ref, lnb_ref, wo_ref, o_ref, xa_ext, u_ext, mix, shf):
    i = pl.program_id(1)
    ts = xa_ref.shape[1]
    C = xa_ref.shape[2]
    H = MIX_HALO

    @pl.when(i == 0)
    def _():
        xa_ext[0:H, :] = jnp.zeros((H, C), F32)
        u_ext[0:H, :] = jnp.zeros((H, C), F32)

    @pl.when(i > 0)
    def _():
        xa_ext[0:H, :] = xah_ref[0].astype(F32)
        u_ext[0:H, :] = uh_ref[0].astype(F32)

    xa_ext[H:H + ts, :] = xa_ref[0].astype(F32)
    u_ext[H:H + ts, :] = u_ref[0].astype(F32)

    t_glob = i * ts + lax.broadcasted_iota(I32, (ts, 1), 0)
    for g, win in enumerate(POOL_WINDOWS):
        lo = g * LANES
        acc = xa_ext[H:H + ts, lo:lo + LANES]
        cur = acc
        for j in range(1, win):
            acc = acc + xa_ext[H - j:H - j + ts, lo:lo + LANES]
        cnt = jnp.minimum(t_glob + 1, win).astype(F32)
        pooled = acc / cnt - cur
        y = _dot(pooled.astype(BF16), pw_ref[g]) * ps_ref[:, lo:lo + LANES]
        mix[:, lo:lo + LANES] = y.astype(BF16)

    CH = 64
    SUB = 8
    n_a = (CONV_WIDTH - 1) // SUB + 1
    back = SUB * (n_a - 1)
    for r0 in range(0, ts, CH):
        acc = jnp.zeros((CH, C), F32) + dwb_ref[...]
        for b in range(SUB):
            lo = H - back - b + r0
            shf[...] = u_ext[lo:lo + CH + back, :]
            for a in range(n_a):
                shift = SUB * a + b
                if shift < CONV_WIDTH:
                    k = CONV_WIDTH - 1 - shift
                    acc = acc + dww_ref[k:k + 1, :] * shf[back - SUB * a:back - SUB * a + CH, :]
        mu = jnp.mean(acc, axis=-1, keepdims=True)
        xc = acc - mu
        var = jnp.mean(xc * xc, axis=-1, keepdims=True)
        yn = xc * lax.rsqrt(var + NORM_EPS) * lng_ref[...] + lnb_ref[...]
        mix[r0:r0 + CH, C:2 * C] = _silu(yn).astype(BF16)

    o_ref[0] = x_ref[0] + _dot(mix[...], wo_ref[...])


def _mix_even(xa, u, x, pw, ps, dww, dwb, lng, lnb, wo, ts=512):
    B, S, C = xa.shape
    D = x.shape[-1]
    ts = min(ts, S)
    hb = ts // MIX_HALO
    main = lambda b, i: (b, i, 0)
    halo = lambda b, i: (b, jnp.maximum(i * hb - 1, 0), 0)
    const2 = lambda b, i: (0, 0)
    return pl.pallas_call(
        _mix_even_kernel,
        grid=(B, S // ts),
        in_specs=[
            pl.BlockSpec((1, ts, C), main),
            pl.BlockSpec((1, MIX_HALO, C), halo),
            pl.BlockSpec((1, ts, C), main),
            pl.BlockSpec((1, MIX_HALO, C), halo),
            pl.BlockSpec((1, ts, D), main),
            pl.BlockSpec(pw.shape, lambda b, i: (0, 0, 0)),
            pl.BlockSpec((1, C), const2),
            pl.BlockSpec(dww.shape, const2),
            pl.BlockSpec((1, C), const2),
            pl.BlockSpec((1, C), const2),
            pl.BlockSpec((1, C), const2),
            pl.BlockSpec(wo.shape, const2),
        ],
        out_specs=pl.BlockSpec((1, ts, D), main),
        out_shape=jax.ShapeDtypeStruct((B, S, D), F32),
        scratch_shapes=[
            pltpu.VMEM((MIX_HALO + ts, C), F32),
            pltpu.VMEM((MIX_HALO + ts, C), F32),
            pltpu.VMEM((ts, 2 * C), BF16),
            pltpu.VMEM((64 + 8 * ((CONV_WIDTH - 1) // 8), C), F32),
        ],
        compiler_params=_cparams("parallel", "arbitrary"),
        name="mix_even",
    )(xa, xa, u, u, x, pw, ps, dww, dwb, lng, lnb, wo)


def _ffn_kernel(x_ref, g_ref, wg_ref, wu_ref, wd_ref, o_ref, h_scr, act_scr, *, fc):
    x = x_ref[...]
    h_scr[...] = _rmsnorm(x, g_ref[...]).astype(BF16)
    F = wg_ref.shape[1]
    for c0 in range(0, F, fc):
        hb = h_scr[...]
        a = _dot(hb, wg_ref[:, c0:c0 + fc])
        b = _dot(hb, wu_ref[:, c0:c0 + fc])
        act_scr[:, c0:c0 + fc] = (_silu(a) * b).astype(BF16)
    o_ref[...] = x + _dot(act_scr[...], wd_ref[...])


def _ffn(x2, g, wg, wu, wd, tm=512, fc=256):
    T, D = x2.shape
    F = wg.shape[1]
    tm = min(tm, T)
    fc = min(fc, F)
    const = lambda i: (0, 0)
    return pl.pallas_call(
        functools.partial(_ffn_kernel, fc=fc),
        grid=(T // tm,),
        in_specs=[
            pl.BlockSpec((tm, D), lambda i: (i, 0)),
            pl.BlockSpec((1, D), const),
            pl.BlockSpec((D, F), const),
            pl.BlockSpec((D, F), const),
            pl.BlockSpec((F, D), const),
        ],
        out_specs=pl.BlockSpec((tm, D), lambda i: (i, 0)),
        out_shape=jax.ShapeDtypeStruct((T, D), F32),
        scratch_shapes=[pltpu.VMEM((tm, D), BF16), pltpu.VMEM((tm, F), BF16)],
        compiler_params=_cparams("parallel"),
        name="ffn",
    )(x2, g, wg, wu, wd)


def _in_odd_kernel(x_ref, g_ref, w_ref, wf_ref, fb_ref, bd_ref, gq_ref, gk_ref,
                   q_ref, k_ref, v_ref, lf_ref, u_ref, sb_ref):
    A = q_ref.shape[-1]
    hb = _rmsnorm(x_ref[...], g_ref[...]).astype(BF16)

    def headnorm(t, gain):
        ss = _dot((t * t).astype(BF16), bd_ref[...])
        return t * lax.rsqrt(ss * (1.0 / HEAD_DIM) + NORM_EPS) * gain

    q = _dot(hb, w_ref[:, 0:A])
    q_ref[...] = (headnorm(q, gq_ref[...]) * (HEAD_DIM ** -0.5 * LOG2E)).astype(BF16)
    k = _dot(hb, w_ref[:, A:2 * A])
    k_ref[...] = headnorm(k, gk_ref[...]).astype(BF16)
    v_ref[...] = _dot(hb, w_ref[:, 2 * A:3 * A]).astype(BF16)
    z = _dot(hb, wf_ref[...]) + fb_ref[...]
    lf_ref[...] = (jnp.minimum(z, 0.0) - jnp.log1p(jnp.exp(-jnp.abs(z)))) * LOG2E
    sx = _dot(hb, w_ref[:, 3 * A:4 * A])
    sb_ref[...] = _dot(hb, w_ref[:, 4 * A:5 * A]).astype(BF16)
    sc = _dot(hb, w_ref[:, 5 * A:6 * A])
    u_ref[...] = (sc * sx).astype(BF16)


def _in_odd(x2, g, w, wf, fb, bd, gq, gk, tm=512):
    T, D = x2.shape
    A = w.shape[1] // 6
    tm = min(tm, T)
    const = lambda i: (0, 0)
    row = lambda i: (i, 0)
    return pl.pallas_call(
        _in_odd_kernel,
        grid=(T // tm,),
        in_specs=[
            pl.BlockSpec((tm, D), row),
            pl.BlockSpec((1, D), const),
            pl.BlockSpec(w.shape, const),
            pl.BlockSpec(wf.shape, const),
            pl.BlockSpec((1, LANES), const),
            pl.BlockSpec(bd.shape, const),
            pl.BlockSpec((1, A), const),
            pl.BlockSpec((1, A), const),
        ],
        out_specs=[
            pl.BlockSpec((tm, A), row), pl.BlockSpec((tm, A), row), pl.BlockSpec((tm, A), row),
            pl.BlockSpec((tm, LANES), row),
            pl.BlockSpec((tm, A), row), pl.BlockSpec((tm, A), row),
        ],
        out_shape=[
            jax.ShapeDtypeStruct((T, A), BF16), jax.ShapeDtypeStruct((T, A), BF16),
            jax.ShapeDtypeStruct((T, A), BF16), jax.ShapeDtypeStruct((T, LANES), F32),
            jax.ShapeDtypeStruct((T, A), BF16), jax.ShapeDtypeStruct((T, A), BF16),
        ],
        compiler_params=_cparams("parallel"),
        name="in_odd",
    )(x2, g, w, wf, fb, bd, gq, gk)


def _attn_kernel(q_ref, k_ref, v_ref, lf_ref, o_ref, qa_scr, ka_scr, *, tq, tk):
    p = pl.program_id(1)
    qi = pl.program_id(2)
    S = k_ref.shape[1]
    lane = lax.broadcasted_iota(I32, (1, LANES), 1)

    def head_lanes(x, hl):
        return x if hl == 0 else pltpu.roll(x, HEAD_DIM, axis=1)

    @pl.when(qi == 0)
    def _():
        f = lf_ref[0]
        row = lax.broadcasted_iota(I32, (S, 1), 0)
        sh = 1
        while sh < S:
            f = f + jnp.where(row >= sh, pltpu.roll(f, sh, axis=0), 0.0)
            sh *= 2
        kf = k_ref[0].astype(F32)
        for hl in range(2):
            hg = 2 * p + hl
            col = jnp.sum(jnp.where(lane == hg, f, 0.0), axis=-1, keepdims=True)
            parts = []
            rem = col
            for _ in range(3):
                part = rem.astype(BF16).astype(F32)
                parts.append(part)
                rem = rem - part
            qa = jnp.zeros((S, LANES), F32)
            ka = head_lanes(kf, hl)
            for n, part in enumerate(parts):
                qa = jnp.where(lane == HEAD_DIM + n, part, qa)
                qa = jnp.where(lane == HEAD_DIM + 3 + n, 1.0, qa)
                ka = jnp.where(lane == HEAD_DIM + n, 1.0, ka)
                ka = jnp.where(lane == HEAD_DIM + 3 + n, -part, ka)
            ka = jnp.where(lane >= HEAD_DIM + 6, 0.0, ka)
            qa_scr[hl] = qa.astype(BF16)
            ka_scr[hl] = ka.astype(BF16)

    q0 = pl.multiple_of(qi * tq, tq)
    qf = q_ref[0].astype(F32)
    rowi = lax.broadcasted_iota(I32, (tq, tk), 0)
    coli = lax.broadcasted_iota(I32, (tq, tk), 1)
    qh = [jnp.where(lane < HEAD_DIM, head_lanes(qf, hl),
                    qa_scr[hl, pl.ds(q0, tq), :].astype(F32)).astype(BF16) for hl in range(2)]

    def scores(hl, k0):
        kb = ka_scr[hl, pl.ds(k0, tk), :]
        return lax.dot_general(qh[hl], kb, (((1,), (1,)), ((), ())), preferred_element_type=F32)

    carry = []
    for hl in range(2):
        s = jnp.where(rowi >= coli, scores(hl, q0), -jnp.inf)
        m = jnp.max(s, axis=-1, keepdims=True)
        pr = jnp.exp2(s - m)
        l = jnp.sum(pr, axis=-1, keepdims=True)
        carry += [m, l, _dot(pr.astype(BF16), v_ref[0, pl.ds(q0, tk), :])]

    def step(k0, carry, mask):
        vb = v_ref[0, pl.ds(k0, tk), :]
        out = []
        for hl in range(2):
            m, l, acc = carry[3 * hl:3 * hl + 3]
            s = scores(hl, k0)
            if mask is not None:
                s = jnp.where(mask, s, -jnp.inf)
            m_new = jnp.maximum(m, jnp.max(s, axis=-1, keepdims=True))
            alpha = jnp.exp2(m - m_new)
            pr = jnp.exp2(s - m_new)
            l = alpha * l + jnp.sum(pr, axis=-1, keepdims=True)
            out += [m_new, l, alpha * acc + _dot(pr.astype(BF16), vb)]
        return tuple(out)

    carry = tuple(carry)
    for d in range(1, tq // tk):
        carry = step(q0 + d * tk, carry, rowi >= coli + d * tk)
    carry = lax.fori_loop(0, qi * (tq // tk),
                          lambda j, c: step(pl.multiple_of(j * tk, tk), c, None), carry)
    outs = [carry[3 * hl + 2] / carry[3 * hl + 1] for hl in range(2)]
    o_ref[0] = jnp.where(lane < HEAD_DIM, outs[0], outs[1]).astype(BF16)


def _attn(q, k, v, lf, tq=512, tk=512):
    B, S, A = q.shape
    tq = min(tq, S)
    tk = min(tk, tq)
    npair = A // LANES
    return pl.pallas_call(
        functools.partial(_attn_kernel, tq=tq, tk=tk),
        grid=(B, npair, S // tq),
        in_specs=[
            pl.BlockSpec((1, tq, LANES), lambda b, p, i: (b, i, p)),
            pl.BlockSpec((1, S, LANES), lambda b, p, i: (b, 0, p)),
            pl.BlockSpec((1, S, LANES), lambda b, p, i: (b, 0, p)),
            pl.BlockSpec((1, S, LANES), lambda b, p, i: (b, 0, 0)),
        ],
        out_specs=pl.BlockSpec((1, tq, LANES), lambda b, p, i: (b, i, p)),
        out_shape=jax.ShapeDtypeStruct((B, S, A), BF16),
        scratch_shapes=[pltpu.VMEM((2, S, LANES), BF16), pltpu.VMEM((2, S, LANES), BF16)],
        compiler_params=_cparams("parallel", "parallel", "arbitrary"),
        name="attn",
    )(q, k, v, lf)


def _out_odd_kernel(yc_ref, u_ref, uh_ref, sb_ref, x_ref, cw_ref, wo_ref, o_ref, u_ext, mix):
    i = pl.program_id(1)
    ts = u_ref.shape[1]
    C = u_ref.shape[2]
    H = SC_HALO

    @pl.when(i == 0)
    def _():
        u_ext[0:H, :] = jnp.zeros((H, C), F32)

    @pl.when(i > 0)
    def _():
        u_ext[0:H, :] = uh_ref[0].astype(F32)

    u_ext[H:H + ts, :] = u_ref[0].astype(F32)
    conv = jnp.zeros((ts, C), F32)
    for k in range(SC_WIDTH):
        off = H - (SC_WIDTH - 1) + k
        conv = conv + cw_ref[k:k + 1, :] * u_ext[off:off + ts, :]
    mix[:, 0:C] = yc_ref[0]
    mix[:, C:2 * C] = (sb_ref[0].astype(F32) * conv).astype(BF16)
    o_ref[0] = x_ref[0] + _dot(mix[...], wo_ref[...])


def _out_odd(yc, u, sb, x, cw, wo, ts=512):
    B, S, C = u.shape
    D = x.shape[-1]
    ts = min(ts, S)
    hb = ts // SC_HALO
    main = lambda b, i: (b, i, 0)
    halo = lambda b, i: (b, jnp.maximum(i * hb - 1, 0), 0)
    const2 = lambda b, i: (0, 0)
    return pl.pallas_call(
        _out_odd_kernel,
        grid=(B, S // ts),
        in_specs=[
            pl.BlockSpec((1, ts, C), main),
            pl.BlockSpec((1, ts, C), main),
            pl.BlockSpec((1, SC_HALO, C), halo),
            pl.BlockSpec((1, ts, C), main),
            pl.BlockSpec((1, ts, D), main),
            pl.BlockSpec(cw.shape, const2),
            pl.BlockSpec(wo.shape, const2),
        ],
        out_specs=pl.BlockSpec((1, ts, D), main),
        out_shape=jax.ShapeDtypeStruct((B, S, D), F32),
        scratch_shapes=[pltpu.VMEM((SC_HALO + ts, C), F32), pltpu.VMEM((ts, 2 * C), BF16)],
        compiler_params=_cparams("parallel", "arbitrary"),
        name="out_odd",
    )(yc, u, u, sb, x, cw, wo)


def _router_kernel(x_ref, g_ref, rw_ref, rb_ref, meta_ref, cnt_ref, base_scr, *, n_exp):
    i = pl.program_id(0)
    tm = x_ref.shape[0]

    @pl.when(i == 0)
    def _():
        base_scr[...] = jnp.zeros_like(base_scr)

    h = _rmsnorm(x_ref[...], g_ref[...])
    lane = lax.broadcasted_iota(I32, (1, LANES), 1)
    h_hi = h.astype(BF16)
    h_lo = (h - h_hi.astype(F32)).astype(BF16)
    logits = _dot(h_hi, rw_ref[0]) + _dot(h_hi, rw_ref[1]) + _dot(h_lo, rw_ref[0])
    logits = jnp.where(lane < n_exp, logits + rb_ref[...], -jnp.inf)

    m1 = jnp.max(logits, axis=-1, keepdims=True)
    i1 = jnp.min(jnp.where(logits == m1, lane, LANES), axis=-1, keepdims=True)
    rest = jnp.where(lane == i1, -jnp.inf, logits)
    m2 = jnp.max(rest, axis=-1, keepdims=True)
    i2 = jnp.min(jnp.where(rest == m2, lane, LANES), axis=-1, keepdims=True)
    e2 = jnp.exp(m2 - m1)
    g1 = 1.0 / (1.0 + e2)
    g2 = e2 / (1.0 + e2)

    oh1 = (lane == i1)
    oh2 = (lane == i2)
    sel = jnp.where(oh1 | oh2, 1.0, 0.0).astype(BF16)
    r = lax.broadcasted_iota(I32, (tm, tm), 0)
    c = lax.broadcasted_iota(I32, (tm, tm), 1)
    strict = jnp.where(r > c, 1.0, 0.0).astype(BF16)
    before = _dot(strict, sel) + base_scr[...]
    r1 = jnp.sum(jnp.where(oh1, before, 0.0), axis=-1, keepdims=True)
    r2 = jnp.sum(jnp.where(oh2, before, 0.0), axis=-1, keepdims=True)
    base_scr[...] = base_scr[...] + jnp.sum(sel.astype(F32), axis=0, keepdims=True)

    meta = jnp.zeros((tm, LANES), F32)
    for idx, val in enumerate((i1.astype(F32), i2.astype(F32), g1, g2, r1, r2)):
        meta = jnp.where(lane == idx, val, meta)
    meta_ref[...] = meta
    cnt_ref[...] = base_scr[...]


def _router(x2, g, rwt, rb, n_exp, tm=512):
    T, D = x2.shape
    tm = min(tm, T)
    const = lambda i: (0, 0)
    return pl.pallas_call(
        functools.partial(_router_kernel, n_exp=n_exp),
        grid=(T // tm,),
        in_specs=[
            pl.BlockSpec((tm, D), lambda i: (i, 0)),
            pl.BlockSpec((1, D), const),
            pl.BlockSpec(rwt.shape, lambda i: (0, 0, 0)),
            pl.BlockSpec((1, LANES), const),
        ],
        out_specs=[pl.BlockSpec((tm, LANES), lambda i: (i, 0)), pl.BlockSpec((1, LANES), const)],
        out_shape=[jax.ShapeDtypeStruct((T, LANES), F32), jax.ShapeDtypeStruct((1, LANES), F32)],
        scratch_shapes=[pltpu.VMEM((1, LANES), F32)],
        compiler_params=_cparams("arbitrary"),
        name="router",
    )(x2, g, rwt, rb)


def _dispatch_kernel(pos_ref, pad_ref, x_ref, xs_hbm, zbuf, sem, zsem, *, n_exp, tile):
    i = pl.program_id(0)
    tc = x_ref.shape[0] * SUBLANES
    T = pos_ref.shape[0] // TOP_K

    def zero_copies(e):
        start, head, rem = pad_ref[3 * e], pad_ref[3 * e + 1], pad_ref[3 * e + 2]
        out = []
        for r in range(7):
            out.append((r < head, pltpu.make_async_copy(
                zbuf.at[pl.ds(0, 1)], xs_hbm.at[pl.ds(start + r, 1)], zsem)))
        for b in range(3, (tile - 1).bit_length()):
            n = 1 << b
            dst = pl.multiple_of(start + head + (rem & (n - 1)), 8)
            out.append((((rem >> b) & 1) == 1, pltpu.make_async_copy(
                zbuf.at[pl.ds(0, n)], xs_hbm.at[pl.ds(dst, n)], zsem)))
        return out

    def tail_copies():
        zr = zbuf.shape[0]
        n_active = pad_ref[3 * n_exp]
        out = []
        for t in range(xs_hbm.shape[0] // tile):
            for r0 in range(0, tile, zr):
                out.append((t >= n_active, pltpu.make_async_copy(
                    zbuf, xs_hbm.at[pl.ds(t * tile + r0, zr)], zsem)))
        return out

    def all_zero_copies():
        return [pc for e in range(n_exp) for pc in zero_copies(e)] + tail_copies()

    @pl.when(i == 0)
    def _():
        zbuf[...] = jnp.zeros_like(zbuf)
        for pred, cp in all_zero_copies():
            @pl.when(pred)
            def _():
                cp.start()

    base = i * tc

    def group(g, _):
        for s in range(SUBLANES):
            for k in range(TOP_K):
                slot = pos_ref[k * T + base + g * SUBLANES + s]
                pltpu.make_async_copy(x_ref.at[g, pl.ds(s, 1)], xs_hbm.at[pl.ds(slot, 1)], sem).start()
        return 0

    lax.fori_loop(0, tc // SUBLANES, group, 0)
    for k in range(TOP_K):
        pltpu.make_async_copy(x_ref, x_ref, sem).wait()

    @pl.when(i == pl.num_programs(0) - 1)
    def _():
        for pred, cp in all_zero_copies():
            @pl.when(pred)
            def _():
                cp.wait()


def _dispatch(pos, pad, x2, n_rows, n_exp, tile, tc=512):
    T, D = x2.shape
    tc = min(tc, T)
    return pl.pallas_call(
        functools.partial(_dispatch_kernel, n_exp=n_exp, tile=tile),
        grid_spec=pltpu.PrefetchScalarGridSpec(
            num_scalar_prefetch=2,
            grid=(T // tc,),
            in_specs=[pl.BlockSpec((tc // SUBLANES, SUBLANES, D), lambda i, pos, pad: (i, 0, 0))],
            out_specs=pl.BlockSpec(memory_space=pl.ANY),
            scratch_shapes=[
                pltpu.VMEM((max(tile // 2, 8), D), F32),
                pltpu.SemaphoreType.DMA,
                pltpu.SemaphoreType.DMA,
            ],
        ),
        out_shape=jax.ShapeDtypeStruct((n_rows, D), F32),
        compiler_params=_cparams("arbitrary"),
        name="dispatch",
    )(pos, pad, x2.reshape(T // SUBLANES, SUBLANES, D))


def _experts_kernel(te_ref, na_ref, tr_ref, xs_ref, g_ref, wg_ref, wu_ref, wd_ref, y_ref, h_scr, acc_scr):
    t = pl.program_id(0)
    j = pl.program_id(1)
    nj = pl.num_programs(1)
    tile = xs_ref.shape[0]
    half = tile // 2

    def ff_step(rows):
        hb = h_scr[0:rows, :]
        a = _dot(hb, wg_ref[0, 0].astype(BF16))
        b = _dot(hb, wu_ref[0, 0].astype(BF16))
        act = (_silu(a) * b).astype(BF16)
        acc_scr[0:rows, :] += _dot(act, wd_ref[0, 0].astype(BF16))

    @pl.when(t < na_ref[0])
    def _():
        @pl.when(j == 0)
        def _():
            h_scr[...] = _rmsnorm(xs_ref[...], g_ref[...]).astype(BF16)
            acc_scr[...] = jnp.zeros_like(acc_scr)

        @pl.when(tr_ref[t] > half)
        def _():
            ff_step(tile)

        @pl.when(tr_ref[t] <= half)
        def _():
            ff_step(half)

        @pl.when(j == nj - 1)
        def _():
            y_ref[...] = acc_scr[...]

    @pl.when((t >= na_ref[0]) & (j == 0))
    def _():
        y_ref[...] = jnp.zeros_like(y_ref)


def _experts(tile_exp, n_active, tile_rows, xs, g, wg, wu, wd, layer, tile, tf):
    P, D = xs.shape
    F = wg.shape[-1]
    tf = min(tf, F)
    nj = F // tf
    n_tiles = P // tile

    def row_map(t, j, te, na, tr):
        return (jnp.minimum(t, na[0] - 1), 0)

    def jeff(t, j, na):
        return jnp.where(t < na[0], j, nj - 1)

    return pl.pallas_call(
        _experts_kernel,
        grid_spec=pltpu.PrefetchScalarGridSpec(
            num_scalar_prefetch=3,
            grid=(n_tiles, nj),
            in_specs=[
                pl.BlockSpec((tile, D), row_map),
                pl.BlockSpec((1, D), lambda t, j, te, na, tr: (0, 0)),
                pl.BlockSpec((1, 1, D, tf), lambda t, j, te, na, tr: (layer, te[t], 0, jeff(t, j, na))),
                pl.BlockSpec((1, 1, D, tf), lambda t, j, te, na, tr: (layer, te[t], 0, jeff(t, j, na))),
                pl.BlockSpec((1, 1, tf, D), lambda t, j, te, na, tr: (layer, te[t], jeff(t, j, na), 0)),
            ],
            out_specs=pl.BlockSpec((tile, D), lambda t, j, te, na, tr: (t, 0)),
            scratch_shapes=[pltpu.VMEM((tile, D), BF16), pltpu.VMEM((tile, D), F32)],
        ),
        out_shape=jax.ShapeDtypeStruct((P, D), F32),
        compiler_params=_cparams("arbitrary", "arbitrary"),
        name="experts",
    )(tile_exp, n_active, tile_rows, xs, g, wg, wu, wd)


def _combine_kernel(pos_ref, x_ref, meta_ref, y_hbm, o_ref, buf, sem):
    i = pl.program_id(0)
    tc = x_ref.shape[0]
    T = pos_ref.shape[0] // TOP_K

    def start_gather(step, b):
        def group(g, _):
            for s in range(SUBLANES):
                for k in range(TOP_K):
                    slot = pos_ref[k * T + step * tc + g * SUBLANES + s]
                    pltpu.make_async_copy(
                        y_hbm.at[pl.ds(slot, 1)], buf.at[b, k, g, pl.ds(s, 1)], sem.at[b]).start()
            return 0

        lax.fori_loop(0, tc // SUBLANES, group, 0)

    b = i % 2

    @pl.when(i == 0)
    def _():
        start_gather(0, 0)

    @pl.when(i + 1 < pl.num_programs(0))
    def _():
        start_gather(i + 1, 1 - b)

    for k in range(TOP_K):
        pltpu.make_async_copy(buf.at[b, k], buf.at[b, k], sem.at[b]).wait()
    meta = meta_ref[...]
    y0 = buf[b, 0].reshape(tc, -1)
    y1 = buf[b, 1].reshape(tc, -1)
    o_ref[...] = x_ref[...] + meta[:, 2:3] * y0 + meta[:, 3:4] * y1


def _combine(pos, x2, meta, y, tc=512):
    T, D = x2.shape
    tc = min(tc, T)
    return pl.pallas_call(
        _combine_kernel,
        grid_spec=pltpu.PrefetchScalarGridSpec(
            num_scalar_prefetch=1,
            grid=(T // tc,),
            in_specs=[
                pl.BlockSpec((tc, D), lambda i, pos: (i, 0)),
                pl.BlockSpec((tc, LANES), lambda i, pos: (i, 0)),
                pl.BlockSpec(memory_space=pl.ANY),
            ],
            out_specs=pl.BlockSpec((tc, D), lambda i, pos: (i, 0)),
            scratch_shapes=[pltpu.VMEM((2, TOP_K, tc // SUBLANES, SUBLANES, D), F32),
                            pltpu.SemaphoreType.DMA((2,))],
        ),
        out_shape=jax.ShapeDtypeStruct((T, D), F32),
        compiler_params=_cparams("arbitrary"),
        name="combine",
    )(pos, x2, meta, y)


def _moe(x2, g, rw, rb, wg, wu, wd, layer):
    T, D = x2.shape
    E = rw.shape[1]
    tile = min(MOE_TILE, T)
    n_tiles = (T * TOP_K) // tile + E
    rwp = jnp.zeros((D, LANES), F32).at[:, :E].set(rw)
    rw_hi = rwp.astype(BF16)
    rwt = jnp.stack([rw_hi, (rwp - rw_hi.astype(F32)).astype(BF16)])
    rbp = jnp.zeros((1, LANES), F32).at[0, :E].set(rb)
    meta, cnt = _router(x2, g, rwt, rbp, E)

    counts = cnt[0, :E].astype(I32)
    tiles_e = (counts + tile - 1) // tile
    tile_end = jnp.cumsum(tiles_e)
    row_off = (tile_end - tiles_e) * tile
    n_active = tile_end[-1:]
    tile_ids = jnp.arange(n_tiles, dtype=I32)
    tile_exp = jnp.minimum(jnp.sum(tile_ids[:, None] >= tile_end[None, :], axis=1), E - 1)
    tile_exp = jnp.where(jnp.arange(n_tiles) < n_active[0], tile_exp, tile_exp[jnp.maximum(n_active[0] - 1, 0)])
    pad_start = row_off + counts
    pad_head = (-pad_start) & 7
    pad_rest = tiles_e * tile - counts - pad_head
    pad = jnp.concatenate([jnp.stack([pad_start, pad_head, pad_rest], axis=1).reshape(-1), n_active]).astype(I32)
    eid = meta[:, 0:TOP_K].astype(I32)
    rank = meta[:, 4:4 + TOP_K].astype(I32)
    pos = (row_off[eid] + rank).T.reshape(-1)

    xs = _dispatch(pos, pad, x2, n_tiles * tile, E, tile)
    tile_start = (tile_end - tiles_e)[tile_exp]
    tile_rows = jnp.clip(counts[tile_exp] - (tile_ids - tile_start) * tile, 0, tile)
    y = _experts(tile_exp.astype(I32), n_active.astype(I32), tile_rows.astype(I32),
                 xs, g, wg, wu, wd, layer, tile, MOE_FF_TILE)
    return _combine(pos, x2, meta, y)


def kernel(x, ev_norm_mix, ev_w_in, pool_w, pool_scale, conv_dw_w, conv_dw_b, conv_ln_g, conv_ln_b, ev_w_out, ev_norm_ffn, ffn_w_gate, ffn_w_up, ffn_w_down, od_norm_mix, od_w_in, forget_bias, q_norm_g, k_norm_g, sc_conv_w, od_w_out, od_norm_ffn, router_w, router_b, moe_w_gate, moe_w_up, moe_w_down):
    B, S, D = x.shape
    T = B * S
    depth = ev_norm_mix.shape[0] + od_norm_mix.shape[0]
    A = od_w_in.shape[2] * HEAD_DIM // (6 * HEAD_DIM + 1)
    H = A // HEAD_DIM
    row = lambda v: v.reshape(1, -1)
    head = jnp.arange(A) // HEAD_DIM
    bd = (head[:, None] == head[None, :]).astype(BF16)

    x2 = x.reshape(T, D)
    for layer in range(depth):
        i = layer // 2
        if layer % 2 == 0:
            C = pool_scale.shape[1]
            xa, u = _in_even(x2, row(ev_norm_mix[i]), ev_w_in[i].astype(BF16))
            x3 = _mix_even(
                xa.reshape(B, S, C), u.reshape(B, S, C), x2.reshape(B, S, D),
                pool_w[i].astype(BF16), row(pool_scale[i]), conv_dw_w[i], row(conv_dw_b[i]),
                row(conv_ln_g[i]), row(conv_ln_b[i]), ev_w_out[i].astype(BF16))
            x2 = _ffn(x3.reshape(T, D), row(ev_norm_ffn[i]), ffn_w_gate[i].astype(BF16),
                      ffn_w_up[i].astype(BF16), ffn_w_down[i].astype(BF16))
        else:
            w = od_w_in[i]
            w_main = jnp.concatenate([w[:, :3 * A], w[:, 3 * A + H:]], axis=1).astype(BF16)
            w_f = jnp.zeros((D, LANES), F32).at[:, :H].set(w[:, 3 * A:3 * A + H]).astype(BF16)
            fb = jnp.zeros((1, LANES), F32).at[0, :H].set(forget_bias[i])
            q, k, v, lf, u, sb = _in_odd(
                x2, row(od_norm_mix[i]), w_main, w_f, fb, bd,
                row(jnp.tile(q_norm_g[i], H)), row(jnp.tile(k_norm_g[i], H)))
            sh = lambda t: t.reshape(B, S, -1)
            yc = _attn(sh(q), sh(k), sh(v), sh(lf))
            x3 = _out_odd(yc, sh(u), sh(sb), x2.reshape(B, S, D), sc_conv_w[i], od_w_out[i].astype(BF16))
            x2 = _moe(x3.reshape(T, D), row(od_norm_ffn[i]), router_w[i], router_b[i],
                      moe_w_gate, moe_w_up, moe_w_down, i)
    return x2.reshape(B, S, D)
```

```python
import functools

import jax
import jax.numpy as jnp
from jax import lax
from jax.experimental import pallas as pl
from jax.experimental.pallas import tpu as pltpu

F32 = jnp.float32
BF16 = jnp.bfloat16
I32 = jnp.int32

NORM_EPS = 1e-6
LANES = 128
SUBLANES = 8
POOL_WINDOWS = (2, 4, 8, 16)
CONV_WIDTH = 31
SC_WIDTH = 3
HEAD_DIM = 64
TOP_K = 2
LOG2E = 1.4426950408889634
MIX_HALO = 32
SC_HALO = 16
MOE_TILE = 1024
MOE_FF_TILE = 512
VMEM_LIMIT = 56 * 1024 * 1024


def _cparams(*sem):
    return pltpu.CompilerParams(dimension_semantics=sem, vmem_limit_bytes=VMEM_LIMIT)


def _rmsnorm(x, g):
    return x * lax.rsqrt(jnp.mean(x * x, axis=-1, keepdims=True) + NORM_EPS) * g


def _sigmoid(x):
    return 1.0 / (1.0 + jnp.exp(-x))


def _silu(x):
    return x * _sigmoid(x)


def _dot(a, b):
    return jnp.dot(a, b, preferred_element_type=F32)


def _in_even_kernel(x_ref, g_ref, w_ref, xa_ref, u_ref):
    c = xa_ref.shape[-1]
    hb = _rmsnorm(x_ref[...], g_ref[...]).astype(BF16)
    xa_ref[...] = _dot(hb, w_ref[:, 0:c]).astype(BF16)
    val = _dot(hb, w_ref[:, c:2 * c])
    gate = _dot(hb, w_ref[:, 2 * c:3 * c])
    u_ref[...] = (val * _sigmoid(gate)).astype(BF16)


def _in_even(x2, g, w, tm=512):
    T, D = x2.shape
    C = w.shape[1] // 3
    tm = min(tm, T)
    return pl.pallas_call(
        _in_even_kernel,
        grid=(T // tm,),
        in_specs=[
            pl.BlockSpec((tm, D), lambda i: (i, 0)),
            pl.BlockSpec((1, D), lambda i: (0, 0)),
            pl.BlockSpec((D, 3 * C), lambda i: (0, 0)),
        ],
        out_specs=[
            pl.BlockSpec((tm, C), lambda i: (i, 0)),
            pl.BlockSpec((tm, C), lambda i: (i, 0)),
        ],
        out_shape=[jax.ShapeDtypeStruct((T, C), BF16)] * 2,
        compiler_params=_cparams("parallel"),
        name="in_even",
    )(x2, g, w)


def _mix_even_kernel(xa_ref, xah_ref, u_ref, uh_ref, x_ref, pw_ref, ps_ref, dww_ref,
                     dwb_ref, lng_ref, lnb_ref, wo_ref, o_ref, xa_ext, u_ext, mix, shf):
    i = pl.program_id(1)
    ts = xa_ref.shape[1]
    C = xa_ref.shape[2]
    H = MIX_HALO

    @pl.when(i == 0)
    def _():
        xa_ext[0:H, :] = jnp.zeros((H, C), F32)
        u_ext[0:H, :] = jnp.zeros((H, C), F32)

    @pl.when(i > 0)
    def _():
        xa_ext[0:H, :] = xah_ref[0].astype(F32)
        u_ext[0:H, :] = uh_ref[0].astype(F32)

    xa_ext[H:H + ts, :] = xa_ref[0].astype(F32)
    u_ext[H:H + ts, :] = u_ref[0].astype(F32)

    t_glob = i * ts + lax.broadcasted_iota(I32, (ts, 1), 0)
    for g, win in enumerate(POOL_WINDOWS):
        lo = g * LANES
        acc = xa_ext[H:H + ts, lo:lo + LANES]
        cur = acc
        for j in range(1, win):
            acc = acc + xa_ext[H - j:H - j + ts, lo:lo + LANES]
        cnt = jnp.minimum(t_glob + 1, win).astype(F32)
        pooled = acc / cnt - cur
        y = _dot(pooled.astype(BF16), pw_ref[g]) * ps_ref[:, lo:lo + LANES]
        mix[:, lo:lo + LANES] = y.astype(BF16)

    CH = 64
    SUB = 8
    n_a = (CONV_WIDTH - 1) // SUB + 1
    back = SUB * (n_a - 1)
    for r0 in range(0, ts, CH):
        acc = jnp.zeros((CH, C), F32) + dwb_ref[...]
        for b in range(SUB):
            lo = H - back - b + r0
            shf[...] = u_ext[lo:lo + CH + back, :]
            for a in range(n_a):
                shift = SUB * a + b
                if shift < CONV_WIDTH:
                    k = CONV_WIDTH - 1 - shift
                    acc = acc + dww_ref[k:k + 1, :] * shf[back - SUB * a:back - SUB * a + CH, :]
        mu = jnp.mean(acc, axis=-1, keepdims=True)
        xc = acc - mu
        var = jnp.mean(xc * xc, axis=-1, keepdims=True)
        yn = xc * lax.rsqrt(var + NORM_EPS) * lng_ref[...] + lnb_ref[...]
        mix[r0:r0 + CH, C:2 * C] = _silu(yn).astype(BF16)

    o_ref[0] = x_ref[0] + _dot(mix[...], wo_ref[...])


def _mix_even(xa, u, x, pw, ps, dww, dwb, lng, lnb, wo, ts=512):
    B, S, C = xa.shape
    D = x.shape[-1]
    ts = min(ts, S)
    hb = ts // MIX_HALO
    main = lambda b, i: (b, i, 0)
    halo = lambda b, i: (b, jnp.maximum(i * hb - 1, 0), 0)
    const2 = lambda b, i: (0, 0)
    return pl.pallas_call(
        _mix_even_kernel,
        grid=(B, S // ts),
        in_specs=[
            pl.BlockSpec((1, ts, C), main),
            pl.BlockSpec((1, MIX_HALO, C), halo),
            pl.BlockSpec((1, ts, C), main),
            pl.BlockSpec((1, MIX_HALO, C), halo),
            pl.BlockSpec((1, ts, D), main),
            pl.BlockSpec(pw.shape, lambda b, i: (0, 0, 0)),
            pl.BlockSpec((1, C), const2),
            pl.BlockSpec(dww.shape, const2),
            pl.BlockSpec((1, C), const2),
            pl.BlockSpec((1, C), const2),
            pl.BlockSpec((1, C), const2),
            pl.BlockSpec(wo.shape, const2),
        ],
        out_specs=pl.BlockSpec((1, ts, D), main),
        out_shape=jax.ShapeDtypeStruct((B, S, D), F32),
        scratch_shapes=[
            pltpu.VMEM((MIX_HALO + ts, C), F32),
            pltpu.VMEM((MIX_HALO + ts, C), F32),
            pltpu.VMEM((ts, 2 * C), BF16),
            pltpu.VMEM((64 + 8 * ((CONV_WIDTH - 1) // 8), C), F32),
        ],
        compiler_params=_cparams("parallel", "arbitrary"),
        name="mix_even",
    )(xa, xa, u, u, x, pw, ps, dww, dwb, lng, lnb, wo)


def _ffn_kernel(x_ref, g_ref, wg_ref, wu_ref, wd_ref, o_ref, h_scr, act_scr, *, fc):
    x = x_ref[...]
    h_scr[...] = _rmsnorm(x, g_ref[...]).astype(BF16)
    F = wg_ref.shape[1]
    for c0 in range(0, F, fc):
        hb = h_scr[...]
        a = _dot(hb, wg_ref[:, c0:c0 + fc])
        b = _dot(hb, wu_ref[:, c0:c0 + fc])
        act_scr[:, c0:c0 + fc] = (_silu(a) * b).astype(BF16)
    o_ref[...] = x + _dot(act_scr[...], wd_ref[...])


def _ffn(x2, g, wg, wu, wd, tm=512, fc=256):
    T, D = x2.shape
    F = wg.shape[1]
    tm = min(tm, T)
    fc = min(fc, F)
    const = lambda i: (0, 0)
    return pl.pallas_call(
        functools.partial(_ffn_kernel, fc=fc),
        grid=(T // tm,),
        in_specs=[
            pl.BlockSpec((tm, D), lambda i: (i, 0)),
            pl.BlockSpec((1, D), const),
            pl.BlockSpec((D, F), const),
            pl.BlockSpec((D, F), const),
            pl.BlockSpec((F, D), const),
        ],
        out_specs=pl.BlockSpec((tm, D), lambda i: (i, 0)),
        out_shape=jax.ShapeDtypeStruct((T, D), F32),
        scratch_shapes=[pltpu.VMEM((tm, D), BF16), pltpu.VMEM((tm, F), BF16)],
        compiler_params=_cparams("parallel"),
        name="ffn",
    )(x2, g, wg, wu, wd)


def _in_odd_kernel(x_ref, g_ref, w_ref, wf_ref, fb_ref, bd_ref, gq_ref, gk_ref,
                   q_ref, k_ref, v_ref, lf_ref, u_ref, sb_ref):
    A = q_ref.shape[-1]
    hb = _rmsnorm(x_ref[...], g_ref[...]).astype(BF16)

    def headnorm(t, gain):
        ss = _dot((t * t).astype(BF16), bd_ref[...])
        return t * lax.rsqrt(ss * (1.0 / HEAD_DIM) + NORM_EPS) * gain

    q = _dot(hb, w_ref[:, 0:A])
    q_ref[...] = (headnorm(q, gq_ref[...]) * (HEAD_DIM ** -0.5 * LOG2E)).astype(BF16)
    k = _dot(hb, w_ref[:, A:2 * A])
    k_ref[...] = headnorm(k, gk_ref[...]).astype(BF16)
    v_ref[...] = _dot(hb, w_ref[:, 2 * A:3 * A]).astype(BF16)
    z = _dot(hb, wf_ref[...]) + fb_ref[...]
    lf_ref[...] = (jnp.minimum(z, 0.0) - jnp.log1p(jnp.exp(-jnp.abs(z)))) * LOG2E
    sx = _dot(hb, w_ref[:, 3 * A:4 * A])
    sb_ref[...] = _dot(hb, w_ref[:, 4 * A:5 * A]).astype(BF16)
    sc = _dot(hb, w_ref[:, 5 * A:6 * A])
    u_ref[...] = (sc * sx).astype(BF16)


def _in_odd(x2, g, w, wf, fb, bd, gq, gk, tm=512):
    T, D = x2.shape
    A = w.shape[1] // 6
    tm = min(tm, T)
    const = lambda i: (0, 0)
    row = lambda i: (i, 0)
    return pl.pallas_call(
        _in_odd_kernel,
        grid=(T // tm,),
        in_specs=[
            pl.BlockSpec((tm, D), row),
            pl.BlockSpec((1, D), const),
            pl.BlockSpec(w.shape, const),
            pl.BlockSpec(wf.shape, const),
            pl.BlockSpec((1, LANES), const),
            pl.BlockSpec(bd.shape, const),
            pl.BlockSpec((1, A), const),
            pl.BlockSpec((1, A), const),
        ],
        out_specs=[
            pl.BlockSpec((tm, A), row), pl.BlockSpec((tm, A), row), pl.BlockSpec((tm, A), row),
            pl.BlockSpec((tm, LANES), row),
            pl.BlockSpec((tm, A), row), pl.BlockSpec((tm, A), row),
        ],
        out_shape=[
            jax.ShapeDtypeStruct((T, A), BF16), jax.ShapeDtypeStruct((T, A), BF16),
            jax.ShapeDtypeStruct((T, A), BF16), jax.ShapeDtypeStruct((T, LANES), F32),
            jax.ShapeDtypeStruct((T, A), BF16), jax.ShapeDtypeStruct((T, A), BF16),
        ],
        compiler_params=_cparams("parallel"),
        name="in_odd",
    )(x2, g, w, wf, fb, bd, gq, gk)


def _attn_kernel(q_ref, k_ref, v_ref, lf_ref, o_ref, qa_scr, ka_scr, *, tq, tk):
    p = pl.program_id(1)
    qi = pl.program_id(2)
    S = k_ref.shape[1]
    lane = lax.broadcasted_iota(I32, (1, LANES), 1)

    def head_lanes(x, hl):
        return x if hl == 0 else pltpu.roll(x, HEAD_DIM, axis=1)

    @pl.when(qi == 0)
    def _():
        f = lf_ref[0]
        row = lax.broadcasted_iota(I32, (S, 1), 0)
        sh = 1
        while sh < S:
            f = f + jnp.where(row >= sh, pltpu.roll(f, sh, axis=0), 0.0)
            sh *= 2
        kf = k_ref[0].astype(F32)
        for hl in range(2):
            hg = 2 * p + hl
            col = jnp.sum(jnp.where(lane == hg, f, 0.0), axis=-1, keepdims=True)
            parts = []
            rem = col
            for _ in range(3):
                part = rem.astype(BF16).astype(F32)
                parts.append(part)
                rem = rem - part
            qa = jnp.zeros((S, LANES), F32)
            ka = head_lanes(kf, hl)
            for n, part in enumerate(parts):
                qa = jnp.where(lane == HEAD_DIM + n, part, qa)
                qa = jnp.where(lane == HEAD_DIM + 3 + n, 1.0, qa)
                ka = jnp.where(lane == HEAD_DIM + n, 1.0, ka)
                ka = jnp.where(lane == HEAD_DIM + 3 + n, -part, ka)
            ka = jnp.where(lane >= HEAD_DIM + 6, 0.0, ka)
            qa_scr[hl] = qa.astype(BF16)
            ka_scr[hl] = ka.astype(BF16)

    q0 = pl.multiple_of(qi * tq, tq)
    qf = q_ref[0].astype(F32)
    rowi = lax.broadcasted_iota(I32, (tq, tk), 0)
    coli = lax.broadcasted_iota(I32, (tq, tk), 1)
    qh = [jnp.where(lane < HEAD_DIM, head_lanes(qf, hl),
                    qa_scr[hl, pl.ds(q0, tq), :].astype(F32)).astype(BF16) for hl in range(2)]

    def scores(hl, k0):
        kb = ka_scr[hl, pl.ds(k0, tk), :]
        return lax.dot_general(qh[hl], kb, (((1,), (1,)), ((), ())), preferred_element_type=F32)

    carry = []
    for hl in range(2):
        s = jnp.where(rowi >= coli, scores(hl, q0), -jnp.inf)
        m = jnp.max(s, axis=-1, keepdims=True)
        pr = jnp.exp2(s - m)
        l = jnp.sum(pr, axis=-1, keepdims=True)
        carry += [m, l, _dot(pr.astype(BF16), v_ref[0, pl.ds(q0, tk), :])]

    def step(k0, carry, mask):
        vb = v_ref[0, pl.ds(k0, tk), :]
        out = []
        for hl in range(2):
            m, l, acc = carry[3 * hl:3 * hl + 3]
            s = scores(hl, k0)
            if mask is not None:
                s = jnp.where(mask, s, -jnp.inf)
            m_new = jnp.maximum(m, jnp.max(s, axis=-1, keepdims=True))
            alpha = jnp.exp2(m - m_new)
            pr = jnp.exp2(s - m_new)
            l = alpha * l + jnp.sum(pr, axis=-1, keepdims=True)
            out += [m_new, l, alpha * acc + _dot(pr.astype(BF16), vb)]
        return tuple(out)

    carry = tuple(carry)
    for d in range(1, tq // tk):
        carry = step(q0 + d * tk, carry, rowi >= coli + d * tk)
    carry = lax.fori_loop(0, qi * (tq // tk),
                          lambda j, c: step(pl.multiple_of(j * tk, tk), c, None), carry)
    outs = [carry[3 * hl + 2] / carry[3 * hl + 1] for hl in range(2)]
    o_ref[0] = jnp.where(lane < HEAD_DIM, outs[0], outs[1]).astype(BF16)


def _attn(q, k, v, lf, tq=512, tk=512):
    B, S, A = q.shape
    tq = min(tq, S)
    tk = min(tk, tq)
    npair = A // LANES
    return pl.pallas_call(
        functools.partial(_attn_kernel, tq=tq, tk=tk),
        grid=(B, npair, S // tq),
        in_specs=[
            pl.BlockSpec((1, tq, LANES), lambda b, p, i: (b, i, p)),
            pl.BlockSpec((1, S, LANES), lambda b, p, i: (b, 0, p)),
            pl.BlockSpec((1, S, LANES), lambda b, p, i: (b, 0, p)),
            pl.BlockSpec((1, S, LANES), lambda b, p, i: (b, 0, 0)),
        ],
        out_specs=pl.BlockSpec((1, tq, LANES), lambda b, p, i: (b, i, p)),
        out_shape=jax.ShapeDtypeStruct((B, S, A), BF16),
        scratch_shapes=[pltpu.VMEM((2, S, LANES), BF16), pltpu.VMEM((2, S, LANES), BF16)],
        compiler_params=_cparams("parallel", "parallel", "arbitrary"),
        name="attn",
    )(q, k, v, lf)


def _out_odd_kernel(yc_ref, u_ref, uh_ref, sb_ref, x_ref, cw_ref, wo_ref, o_ref, u_ext, mix):
    i = pl.program_id(1)
    ts = u_ref.shape[1]
    C = u_ref.shape[2]
    H = SC_HALO

    @pl.when(i == 0)
    def _():
        u_ext[0:H, :] = jnp.zeros((H, C), F32)

    @pl.when(i > 0)
    def _():
        u_ext[0:H, :] = uh_ref[0].astype(F32)

    u_ext[H:H + ts, :] = u_ref[0].astype(F32)
    conv = jnp.zeros((ts, C), F32)
    for k in range(SC_WIDTH):
        off = H - (SC_WIDTH - 1) + k
        conv = conv + cw_ref[k:k + 1, :] * u_ext[off:off + ts, :]
    mix[:, 0:C] = yc_ref[0]
    mix[:, C:2 * C] = (sb_ref[0].astype(F32) * conv).astype(BF16)
    o_ref[0] = x_ref[0] + _dot(mix[...], wo_ref[...])


def _out_odd(yc, u, sb, x, cw, wo, ts=512):
    B, S, C = u.shape
    D = x.shape[-1]
    ts = min(ts, S)
    hb = ts // SC_HALO
    main = lambda b, i: (b, i, 0)
    halo = lambda b, i: (b, jnp.maximum(i * hb - 1, 0), 0)
    const2 = lambda b, i: (0, 0)
    return pl.pallas_call(
        _out_odd_kernel,
        grid=(B, S // ts),
        in_specs=[
            pl.BlockSpec((1, ts, C), main),
            pl.BlockSpec((1, ts, C), main),
            pl.BlockSpec((1, SC_HALO, C), halo),
            pl.BlockSpec((1, ts, C), main),
            pl.BlockSpec((1, ts, D), main),
            pl.BlockSpec(cw.shape, const2),
            pl.BlockSpec(wo.shape, const2),
        ],
        out_specs=pl.BlockSpec((1, ts, D), main),
        out_shape=jax.ShapeDtypeStruct((B, S, D), F32),
        scratch_shapes=[pltpu.VMEM((SC_HALO + ts, C), F32), pltpu.VMEM((ts, 2 * C), BF16)],
        compiler_params=_cparams("parallel", "arbitrary"),
        name="out_odd",
    )(yc, u, u, sb, x, cw, wo)


def _router_kernel(x_ref, g_ref, rw_ref, rb_ref, meta_ref, cnt_ref, base_scr, *, n_exp):
    i = pl.program_id(0)
    tm = x_ref.shape[0]

    @pl.when(i == 0)
    def _():
        base_scr[...] = jnp.zeros_like(base_scr)

    h = _rmsnorm(x_ref[...], g_ref[...])
    lane = lax.broadcasted_iota(I32, (1, LANES), 1)
    h_hi = h.astype(BF16)
    h_lo = (h - h_hi.astype(F32)).astype(BF16)
    logits = _dot(h_hi, rw_ref[0]) + _dot(h_hi, rw_ref[1]) + _dot(h_lo, rw_ref[0])
    logits = jnp.where(lane < n_exp, logits + rb_ref[...], -jnp.inf)

    m1 = jnp.max(logits, axis=-1, keepdims=True)
    i1 = jnp.min(jnp.where(logits == m1, lane, LANES), axis=-1, keepdims=True)
    rest = jnp.where(lane == i1, -jnp.inf, logits)
    m2 = jnp.max(rest, axis=-1, keepdims=True)
    i2 = jnp.min(jnp.where(rest == m2, lane, LANES), axis=-1, keepdims=True)
    e2 = jnp.exp(m2 - m1)
    g1 = 1.0 / (1.0 + e2)
    g2 = e2 / (1.0 + e2)

    oh1 = (lane == i1)
    oh2 = (lane == i2)
    sel = jnp.where(oh1 | oh2, 1.0, 0.0).astype(BF16)
    r = lax.broadcasted_iota(I32, (tm, tm), 0)
    c = lax.broadcasted_iota(I32, (tm, tm), 1)
    strict = jnp.where(r > c, 1.0, 0.0).astype(BF16)
    before = _dot(strict, sel) + base_scr[...]
    r1 = jnp.sum(jnp.where(oh1, before, 0.0), axis=-1, keepdims=True)
    r2 = jnp.sum(jnp.where(oh2, before, 0.0), axis=-1, keepdims=True)
    base_scr[...] = base_scr[...] + jnp.sum(sel.astype(F32), axis=0, keepdims=True)

    meta = jnp.zeros((tm, LANES), F32)
    for idx, val in enumerate((i1.astype(F32), i2.astype(F32), g1, g2, r1, r2)):
        meta = jnp.where(lane == idx, val, meta)
    meta_ref[...] = meta
    cnt_ref[...] = base_scr[...]


def _router(x2, g, rwt, rb, n_exp, tm=512):
    T, D = x2.shape
    tm = min(tm, T)
    const = lambda i: (0, 0)
    return pl.pallas_call(
        functools.partial(_router_kernel, n_exp=n_exp),
        grid=(T // tm,),
        in_specs=[
            pl.BlockSpec((tm, D), lambda i: (i, 0)),
            pl.BlockSpec((1, D), const),
            pl.BlockSpec(rwt.shape, lambda i: (0, 0, 0)),
            pl.BlockSpec((1, LANES), const),
        ],
        out_specs=[pl.BlockSpec((tm, LANES), lambda i: (i, 0)), pl.BlockSpec((1, LANES), const)],
        out_shape=[jax.ShapeDtypeStruct((T, LANES), F32), jax.ShapeDtypeStruct((1, LANES), F32)],
        scratch_shapes=[pltpu.VMEM((1, LANES), F32)],
        compiler_params=_cparams("arbitrary"),
        name="router",
    )(x2, g, rwt, rb)


def _dispatch_kernel(pos_ref, pad_ref, x_ref, xs_hbm, zbuf, sem, zsem, *, n_exp, tile):
    i = pl.program_id(0)
    tc = x_ref.shape[0] * SUBLANES
    T = pos_ref.shape[0] // TOP_K

    def zero_copies(e):
        start, head, rem = pad_ref[3 * e], pad_ref[3 * e + 1], pad_ref[3 * e + 2]
        out = []
        for r in range(7):
            out.append((r < head, pltpu.make_async_copy(
                zbuf.at[pl.ds(0, 1)], xs_hbm.at[pl.ds(start + r, 1)], zsem)))
        for b in range(3, (tile - 1).bit_length()):
            n = 1 << b
            dst = pl.multiple_of(start + head + (rem & (n - 1)), 8)
            out.append((((rem >> b) & 1) == 1, pltpu.make_async_copy(
                zbuf.at[pl.ds(0, n)], xs_hbm.at[pl.ds(dst, n)], zsem)))
        return out

    def tail_copies():
        zr = zbuf.shape[0]
        n_active = pad_ref[3 * n_exp]
        out = []
        for t in range(xs_hbm.shape[0] // tile):
            for r0 in range(0, tile, zr):
                out.append((t >= n_active, pltpu.make_async_copy(
                    zbuf, xs_hbm.at[pl.ds(t * tile + r0, zr)], zsem)))
        return out

    def all_zero_copies():
        return [pc for e in range(n_exp) for pc in zero_copies(e)] + tail_copies()

    @pl.when(i == 0)
    def _():
        zbuf[...] = jnp.zeros_like(zbuf)
        for pred, cp in all_zero_copies():
            @pl.when(pred)
            def _():
                cp.start()

    base = i * tc

    def group(g, _):
        for s in range(SUBLANES):
            for k in range(TOP_K):
                slot = pos_ref[k * T + base + g * SUBLANES + s]
                pltpu.make_async_copy(
                    x_ref.at[g, pl.ds(s, 1)], xs_hbm.at[pl.ds(slot, 1)], sem).start(priority=k)
        return 0

    lax.fori_loop(0, tc // SUBLANES, group, 0)
    for k in range(TOP_K):
        pltpu.make_async_copy(x_ref, x_ref, sem).wait()

    @pl.when(i == pl.num_programs(0) - 1)
    def _():
        for pred, cp in all_zero_copies():
            @pl.when(pred)
            def _():
                cp.wait()


def _dispatch(pos, pad, x2, n_rows, n_exp, tile, tc=512):
    T, D = x2.shape
    tc = min(tc, T)
    return pl.pallas_call(
        functools.partial(_dispatch_kernel, n_exp=n_exp, tile=tile),
        grid_spec=pltpu.PrefetchScalarGridSpec(
            num_scalar_prefetch=2,
            grid=(T // tc,),
            in_specs=[pl.BlockSpec((tc // SUBLANES, SUBLANES, D), lambda i, pos, pad: (i, 0, 0))],
            out_specs=pl.BlockSpec(memory_space=pl.ANY),
            scratch_shapes=[
                pltpu.VMEM((max(tile // 2, 8), D), F32),
                pltpu.SemaphoreType.DMA,
                pltpu.SemaphoreType.DMA,
            ],
        ),
        out_shape=jax.ShapeDtypeStruct((n_rows, D), F32),
        compiler_params=_cparams("arbitrary"),
        name="dispatch",
    )(pos, pad, x2.reshape(T // SUBLANES, SUBLANES, D))


def _experts_kernel(te_ref, na_ref, tr_ref, xs_ref, g_ref, wg_ref, wu_ref, wd_ref, y_ref, h_scr, acc_scr):
    t = pl.program_id(0)
    j = pl.program_id(1)
    nj = pl.num_programs(1)
    tile = xs_ref.shape[0]
    half = tile // 2

    def ff_step(rows):
        hb = h_scr[0:rows, :]
        a = _dot(hb, wg_ref[0, 0].astype(BF16))
        b = _dot(hb, wu_ref[0, 0].astype(BF16))
        act = (_silu(a) * b).astype(BF16)
        acc_scr[0:rows, :] += _dot(act, wd_ref[0, 0].astype(BF16))

    @pl.when(t < na_ref[0])
    def _():
        @pl.when(j == 0)
        def _():
            h_scr[...] = _rmsnorm(xs_ref[...], g_ref[...]).astype(BF16)
            acc_scr[...] = jnp.zeros_like(acc_scr)

        @pl.when(tr_ref[t] > half)
        def _():
            ff_step(tile)

        @pl.when(tr_ref[t] <= half)
        def _():
            ff_step(half)

        @pl.when(j == nj - 1)
        def _():
            y_ref[...] = acc_scr[...]

    @pl.when((t >= na_ref[0]) & (j == 0))
    def _():
        y_ref[...] = jnp.zeros_like(y_ref)


def _experts(tile_exp, n_active, tile_rows, xs, g, wg, wu, wd, layer, tile, tf):
    P, D = xs.shape
    F = wg.shape[-1]
    tf = min(tf, F)
    nj = F // tf
    n_tiles = P // tile

    def row_map(t, j, te, na, tr):
        return (jnp.minimum(t, na[0] - 1), 0)

    def jeff(t, j, na):
        return jnp.where(t < na[0], j, nj - 1)

    return pl.pallas_call(
        _experts_kernel,
        grid_spec=pltpu.PrefetchScalarGridSpec(
            num_scalar_prefetch=3,
            grid=(n_tiles, nj),
            in_specs=[
                pl.BlockSpec((tile, D), row_map),
                pl.BlockSpec((1, D), lambda t, j, te, na, tr: (0, 0)),
                pl.BlockSpec((1, 1, D, tf), lambda t, j, te, na, tr: (layer, te[t], 0, jeff(t, j, na))),
                pl.BlockSpec((1, 1, D, tf), lambda t, j, te, na, tr: (layer, te[t], 0, jeff(t, j, na))),
                pl.BlockSpec((1, 1, tf, D), lambda t, j, te, na, tr: (layer, te[t], jeff(t, j, na), 0)),
            ],
            out_specs=pl.BlockSpec((tile, D), lambda t, j, te, na, tr: (t, 0)),
            scratch_shapes=[pltpu.VMEM((tile, D), BF16), pltpu.VMEM((tile, D), F32)],
        ),
        out_shape=jax.ShapeDtypeStruct((P, D), F32),
        compiler_params=_cparams("arbitrary", "arbitrary"),
        name="experts",
    )(tile_exp, n_active, tile_rows, xs, g, wg, wu, wd)


def _combine_kernel(pos_ref, x_ref, meta_ref, y_hbm, o_ref, buf, sem):
    i = pl.program_id(0)
    tc = x_ref.shape[0]
    T = pos_ref.shape[0] // TOP_K

    def start_gather(step, b):
        def group(g, _):
            for s in range(SUBLANES):
                for k in range(TOP_K):
                    slot = pos_ref[k * T + step * tc + g * SUBLANES + s]
                    pltpu.make_async_copy(
                        y_hbm.at[pl.ds(slot, 1)], buf.at[b, k, g, pl.ds(s, 1)], sem.at[b]).start(priority=k)
            return 0

        lax.fori_loop(0, tc // SUBLANES, group, 0)

    b = i % 2

    @pl.when(i == 0)
    def _():
        start_gather(0, 0)

    @pl.when(i + 1 < pl.num_programs(0))
    def _():
        start_gather(i + 1, 1 - b)

    for k in range(TOP_K):
        pltpu.make_async_copy(buf.at[b, k], buf.at[b, k], sem.at[b]).wait()
    meta = meta_ref[...]
    y0 = buf[b, 0].reshape(tc, -1)
    y1 = buf[b, 1].reshape(tc, -1)
    o_ref[...] = x_ref[...] + meta[:, 2:3] * y0 + meta[:, 3:4] * y1


def _combine(pos, x2, meta, y, tc=512):
    T, D = x2.shape
    tc = min(tc, T)
    return pl.pallas_call(
        _combine_kernel,
        grid_spec=pltpu.PrefetchScalarGridSpec(
            num_scalar_prefetch=1,
            grid=(T // tc,),
            in_specs=[
                pl.BlockSpec((tc, D), lambda i, pos: (i, 0)),
                pl.BlockSpec((tc, LANES), lambda i, pos: (i, 0)),
                pl.BlockSpec(memory_space=pl.ANY),
            ],
            out_specs=pl.BlockSpec((tc, D), lambda i, pos: (i, 0)),
            scratch_shapes=[pltpu.VMEM((2, TOP_K, tc // SUBLANES, SUBLANES, D), F32),
                            pltpu.SemaphoreType.DMA((2,))],
        ),
        out_shape=jax.ShapeDtypeStruct((T, D), F32),
        compiler_params=_cparams("arbitrary"),
        name="combine",
    )(pos, x2, meta, y)


def _moe(x2, g, rw, rb, wg, wu, wd, layer):
    T, D = x2.shape
    E = rw.shape[1]
    tile = min(MOE_TILE, T)
    n_tiles = (T * TOP_K) // tile + E
    rwp = jnp.zeros((D, LANES), F32).at[:, :E].set(rw)
    rw_hi = rwp.astype(BF16)
    rwt = jnp.stack([rw_hi, (rwp - rw_hi.astype(F32)).astype(BF16)])
    rbp = jnp.zeros((1, LANES), F32).at[0, :E].set(rb)
    meta, cnt = _router(x2, g, rwt, rbp, E)

    counts = cnt[0, :E].astype(I32)
    tiles_e = (counts + tile - 1) // tile
    tile_end = jnp.cumsum(tiles_e)
    row_off = (tile_end - tiles_e) * tile
    n_active = tile_end[-1:]
    tile_ids = jnp.arange(n_tiles, dtype=I32)
    tile_exp = jnp.minimum(jnp.sum(tile_ids[:, None] >= tile_end[None, :], axis=1), E - 1)
    tile_exp = jnp.where(jnp.arange(n_tiles) < n_active[0], tile_exp, tile_exp[jnp.maximum(n_active[0] - 1, 0)])
    pad_start = row_off + counts
    pad_head = (-pad_start) & 7
    pad_rest = tiles_e * tile - counts - pad_head
    pad = jnp.concatenate([jnp.stack([pad_start, pad_head, pad_rest], axis=1).reshape(-1), n_active]).astype(I32)
    eid = meta[:, 0:TOP_K].astype(I32)
    rank = meta[:, 4:4 + TOP_K].astype(I32)
    pos = (row_off[eid] + rank).T.reshape(-1)

    xs = _dispatch(pos, pad, x2, n_tiles * tile, E, tile)
    tile_start = (tile_end - tiles_e)[tile_exp]
    tile_rows = jnp.clip(counts[tile_exp] - (tile_ids - tile_start) * tile, 0, tile)
    y = _experts(tile_exp.astype(I32), n_active.astype(I32), tile_rows.astype(I32),
                 xs, g, wg, wu, wd, layer, tile, MOE_FF_TILE)
    return _combine(pos, x2, meta, y)


def kernel(x, ev_norm_mix, ev_w_in, pool_w, pool_scale, conv_dw_w, conv_dw_b, conv_ln_g, conv_ln_b, ev_w_out, ev_norm_ffn, ffn_w_gate, ffn_w_up, ffn_w_down, od_norm_mix, od_w_in, forget_bias, q_norm_g, k_norm_g, sc_conv_w, od_w_out, od_norm_ffn, router_w, router_b, moe_w_gate, moe_w_up, moe_w_down):
    B, S, D = x.shape
    T = B * S
    depth = ev_norm_mix.shape[0] + od_norm_mix.shape[0]
    A = od_w_in.shape[2] * HEAD_DIM // (6 * HEAD_DIM + 1)
    H = A // HEAD_DIM
    row = lambda v: v.reshape(1, -1)
    head = jnp.arange(A) // HEAD_DIM
    bd = (head[:, None] == head[None, :]).astype(BF16)

    x2 = x.reshape(T, D)
    for layer in range(depth):
        i = layer // 2
        if layer % 2 == 0:
            C = pool_scale.shape[1]
            xa, u = _in_even(x2, row(ev_norm_mix[i]), ev_w_in[i].astype(BF16))
            x3 = _mix_even(
                xa.reshape(B, S, C), u.reshape(B, S, C), x2.reshape(B, S, D),
                pool_w[i].astype(BF16), row(pool_scale[i]), conv_dw_w[i], row(conv_dw_b[i]),
                row(conv_ln_g[i]), row(conv_ln_b[i]), ev_w_out[i].astype(BF16))
            x2 = _ffn(x3.reshape(T, D), row(ev_norm_ffn[i]), ffn_w_gate[i].astype(BF16),
                      ffn_w_up[i].astype(BF16), ffn_w_down[i].astype(BF16))
        else:
            w = od_w_in[i]
            w_main = jnp.concatenate([w[:, :3 * A], w[:, 3 * A + H:]], axis=1).astype(BF16)
            w_f = jnp.zeros((D, LANES), F32).at[:, :H].set(w[:, 3 * A:3 * A + H]).astype(BF16)
            fb = jnp.zeros((1, LANES), F32).at[0, :H].set(forget_bias[i])
            q, k, v, lf, u, sb = _in_odd(
                x2, row(od_norm_mix[i]), w_main, w_f, fb, bd,
                row(jnp.tile(q_norm_g[i], H)), row(jnp.tile(k_norm_g[i], H)))
            sh = lambda t: t.reshape(B, S, -1)
            yc = _attn(sh(q), sh(k), sh(v), sh(lf))
            x3 = _out_odd(yc, sh(u), sh(sb), x2.reshape(B, S, D), sc_conv_w[i], od_w_out[i].astype(BF16))
            x2 = _moe(x3.reshape(T, D), row(od_norm_ffn[i]), router_w[i], router_b[i],
                      moe_w_gate, moe_w_up, moe_w_down, i)
    return x2.reshape(B, S, D)
```
